```python
import jax, jax.numpy as jnp
from jax import lax
import numpy as np

D_MODEL = 1024
BATCH = 16
SEQ = 2048
DEPTH = 1
DEC_BATCH = 128
DEC_SEQ = 8
PAST_LEN = 8192
PAGE_SIZE = 128

HEAD_DIM = 64
ATTN_WIDTH = D_MODEL // 2
N_HEADS = ATTN_WIDTH // HEAD_DIM
POOL_WIDTH = D_MODEL - ATTN_WIDTH
POOL_WINDOWS = (2, 4, 8, 16)
N_POOL_GROUPS = len(POOL_WINDOWS)
POOL_GROUP_WIDTH = POOL_WIDTH // N_POOL_GROUPS
POOL_STATE_LEN = max(POOL_WINDOWS) - 1
MIX_WIDTH = ATTN_WIDTH + POOL_WIDTH
IN_WIDTH = 3 * ATTN_WIDTH + POOL_WIDTH
D_FF = 4 * D_MODEL
N_MOD = 6
Q_BLOCK = 128
RMS_EPS = 1e-6
SB_BIAS_HI = -5.0
SB_BIAS_LO = -8.0

kernel_name = 'hymba_stickbreak_pool_adaln_step'


def _rmsnorm(x, g):
    xf = x.astype(jnp.float32)
    inv = lax.rsqrt(jnp.mean(xf * xf, axis=-1, keepdims=True) + RMS_EPS)
    return (xf * inv).astype(x.dtype) * g


def _stick_breaking_block(q_blk, q_pos_blk, k, v, k_pos, sb_bias):
    z = jnp.einsum('nqhd,nkhd->nhqk', q_blk, k).astype(jnp.float32) * (HEAD_DIM ** -0.5)
    z = z + sb_bias.astype(jnp.float32)[None, :, None, None]
    mask = k_pos[None, :] < q_pos_blk[:, None]
    log_keep = jnp.where(mask, jax.nn.log_sigmoid(-z), 0.0)
    later = lax.cumsum(log_keep, axis=3, reverse=True) - log_keep
    a = jnp.where(mask, jnp.exp(jax.nn.log_sigmoid(z) + later), 0.0)
    return jnp.einsum('nhqk,nkhd->nqhd', a.astype(v.dtype), v)


def _stick_breaking_attention(q, k, v, q_pos, k_pos, sb_bias):
    n, t = q.shape[0], q.shape[1]
    qb = Q_BLOCK if t % Q_BLOCK == 0 else t
    nb = t // qb
    q_blocks = jnp.moveaxis(q.reshape(n, nb, qb, N_HEADS, HEAD_DIM), 1, 0)
    pos_blocks = q_pos.reshape(nb, qb)
    out = lax.map(lambda xs: _stick_breaking_block(xs[0], xs[1], k, v, k_pos, sb_bias), (q_blocks, pos_blocks))
    return jnp.moveaxis(out, 0, 1).reshape(n, t, ATTN_WIDTH)


def _multiscale_pool(u, prev, pos, w_pool, pool_scale):
    n, t, _ = u.shape
    ext = jnp.concatenate([prev.astype(u.dtype), u], axis=1)
    cs = jnp.cumsum(ext.astype(jnp.float32), axis=1)
    cs = jnp.concatenate([jnp.zeros((n, 1, POOL_WIDTH), jnp.float32), cs], axis=1)
    end = cs[:, POOL_STATE_LEN + 1:]
    means = []
    for g, w in enumerate(POOL_WINDOWS):
        lo, hi = g * POOL_GROUP_WIDTH, (g + 1) * POOL_GROUP_WIDTH
        start = cs[:, POOL_STATE_LEN + 1 - w:POOL_STATE_LEN + 1 - w + t, lo:hi]
        cnt = jnp.minimum(pos + 1, w).astype(jnp.float32)[None, :, None]
        means.append((end[..., lo:hi] - start) / cnt)
    mean = jnp.concatenate(means, axis=-1)
    d = (mean - u.astype(jnp.float32)).astype(u.dtype).reshape(n, t, N_POOL_GROUPS, POOL_GROUP_WIDTH)
    out = jnp.einsum('ntgc,gce->ntge', d, w_pool).reshape(n, t, POOL_WIDTH) * pool_scale
    return out, ext[:, -POOL_STATE_LEN:]


def _layer(x, c, pos, k_past, v_past, past_pos, pool_prev,
           norm1_g, norm2_g, w_ada, b_ada, w_in, sb_bias, w_pool, pool_scale, w_o, w_ff1, w_ff2):
    n, t, _ = x.shape
    mod = jax.nn.silu(c) @ w_ada + b_ada
    sh1, sc1, g1, sh2, sc2, g2 = [m[:, None, :] for m in jnp.split(mod, N_MOD, axis=-1)]
    h = _rmsnorm(x, norm1_g) * (1 + sc1) + sh1
    proj = h @ w_in
    q, k, v, u = jnp.split(proj, [ATTN_WIDTH, 2 * ATTN_WIDTH, 3 * ATTN_WIDTH], axis=-1)
    q = q.reshape(n, t, N_HEADS, HEAD_DIM)
    k = k.reshape(n, t, N_HEADS, HEAD_DIM)
    v = v.reshape(n, t, N_HEADS, HEAD_DIM)
    k_all = jnp.concatenate([k_past.astype(k.dtype), k], axis=1)
    v_all = jnp.concatenate([v_past.astype(v.dtype), v], axis=1)
    k_pos = jnp.concatenate([past_pos, pos])
    attn = _stick_breaking_attention(q, k_all, v_all, pos, k_pos, sb_bias)
    pooled, pool_new = _multiscale_pool(u, pool_prev, pos, w_pool, pool_scale)
    x = x + g1 * (jnp.concatenate([attn, pooled], axis=-1) @ w_o)
    h2 = _rmsnorm(x, norm2_g) * (1 + sc2) + sh2
    x = x + g2 * (jnp.square(jax.nn.relu(h2 @ w_ff1)) @ w_ff2)
    return x, k, v, pool_new


def setup_inputs(seed: int = 0) -> dict:
    key = jax.random.key(seed)
    ks = jax.random.split(key, 20)
    n_pages = PAST_LEN // PAGE_SIZE
    n_used = DEC_BATCH * n_pages
    n_phys = n_used + n_used // 4
    f32 = jnp.float32

    def nrm(k, shape):
        return jax.random.normal(k, shape, f32)

    def w(k, shape, fan_in):
        return nrm(k, shape) * (fan_in ** -0.5)

    page_table = jax.random.permutation(ks[0], n_phys)[:n_used].reshape(DEC_BATCH, n_pages).astype(jnp.int32)
    sb_bias = jnp.linspace(SB_BIAS_HI, SB_BIAS_LO, N_HEADS, dtype=f32)[None, :] + 0.1 * nrm(ks[19], (DEPTH, N_HEADS))
    return {
        'x_prompt': nrm(ks[1], (BATCH, SEQ, D_MODEL)),
        'x_sample': nrm(ks[2], (DEC_BATCH, DEC_SEQ, D_MODEL)),
        'c_prompt': nrm(ks[3], (BATCH, D_MODEL)),
        'c_sample': nrm(ks[4], (DEC_BATCH, D_MODEL)),
        'cache_k': nrm(ks[5], (DEPTH, n_phys, PAGE_SIZE, N_HEADS, HEAD_DIM)),
        'cache_v': nrm(ks[6], (DEPTH, n_phys, PAGE_SIZE, N_HEADS, HEAD_DIM)),
        'state_pool': nrm(ks[7], (DEPTH, DEC_BATCH, POOL_STATE_LEN, POOL_WIDTH)),
        'page_table': page_table,
        'norm1_g': 1.0 + 0.05 * nrm(ks[8], (DEPTH, D_MODEL)),
        'norm2_g': 1.0 + 0.05 * nrm(ks[9], (DEPTH, D_MODEL)),
        'w_ada': w(ks[10], (DEPTH, D_MODEL, N_MOD * D_MODEL), D_MODEL),
        'b_ada': 0.02 * nrm(ks[11], (DEPTH, N_MOD * D_MODEL)),
        'w_in': w(ks[12], (DEPTH, D_MODEL, IN_WIDTH), D_MODEL),
        'sb_bias': sb_bias,
        'w_pool': w(ks[13], (DEPTH, N_POOL_GROUPS, POOL_GROUP_WIDTH, POOL_GROUP_WIDTH), POOL_GROUP_WIDTH),
        'pool_scale': 1.0 + 0.05 * nrm(ks[14], (DEPTH, POOL_WIDTH)),
        'w_o': w(ks[15], (DEPTH, MIX_WIDTH, D_MODEL), MIX_WIDTH),
        'w_ff1': w(ks[16], (DEPTH, D_MODEL, D_FF), D_MODEL),
        'w_ff2': w(ks[17], (DEPTH, D_FF, D_MODEL), D_FF),
        'final_g': 1.0 + 0.05 * nrm(ks[18], (D_MODEL,)),
    }


def reference(x_prompt, x_sample, c_prompt, c_sample, cache_k, cache_v, state_pool, page_table,
              norm1_g, norm2_g, w_ada, b_ada, w_in, sb_bias, w_pool, pool_scale, w_o, w_ff1, w_ff2, final_g):
    n_b, seq = x_prompt.shape[0], x_prompt.shape[1]
    n_d, dec_seq = x_sample.shape[0], x_sample.shape[1]
    past_len = page_table.shape[1] * cache_k.shape[2]
    pos_p = jnp.arange(seq, dtype=jnp.int32)
    pos_s = past_len + jnp.arange(dec_seq, dtype=jnp.int32)
    past_pos_s = jnp.arange(past_len, dtype=jnp.int32)
    empty_kv = jnp.zeros((n_b, 0, N_HEADS, HEAD_DIM), x_prompt.dtype)
    empty_pos = jnp.zeros((0,), jnp.int32)
    pool_zero = jnp.zeros((n_b, POOL_STATE_LEN, POOL_WIDTH), x_prompt.dtype)

    xp, xs = x_prompt, x_sample
    kp_l, vp_l, pp_l, ks_l, vs_l, ps_l = [], [], [], [], [], []
    for l in range(DEPTH):
        wts = (norm1_g[l], norm2_g[l], w_ada[l], b_ada[l], w_in[l], sb_bias[l], w_pool[l], pool_scale[l],
               w_o[l], w_ff1[l], w_ff2[l])
        xp, kp, vp, pp = _layer(xp, c_prompt, pos_p, empty_kv, empty_kv, empty_pos, pool_zero, *wts)
        k_past = cache_k[l][page_table].reshape(n_d, past_len, N_HEADS, HEAD_DIM)
        v_past = cache_v[l][page_table].reshape(n_d, past_len, N_HEADS, HEAD_DIM)
        xs, ksn, vsn, psn = _layer(xs, c_sample, pos_s, k_past, v_past, past_pos_s, state_pool[l], *wts)
        kp_l.append(kp); vp_l.append(vp); pp_l.append(pp)
        ks_l.append(ksn); vs_l.append(vsn); ps_l.append(psn)

    y_prompt = _rmsnorm(xp, final_g)
    y_sample = _rmsnorm(xs, final_g)
    new_k_prompt = jnp.stack(kp_l)
    new_v_prompt = jnp.stack(vp_l)
    new_pool_prompt = jnp.stack(pp_l)
    new_k_sample = jnp.stack(ks_l)
    new_v_sample = jnp.stack(vs_l)
    new_pool_sample = jnp.stack(ps_l)
    return (y_prompt, y_sample, new_k_prompt, new_v_prompt, new_pool_prompt, new_k_sample, new_v_sample, new_pool_sample)
```

```python
import functools

import jax
import jax.numpy as jnp
from jax import lax
from jax.experimental import pallas as pl
from jax.experimental.pallas import tpu as pltpu

F32 = jnp.float32
BF16 = jnp.bfloat16

HEAD_DIM = 64
POOL_WINDOWS = (2, 4, 8, 16)
POOL_HALO = 16
RMS_EPS = 1e-6
LANES = 128
VMEM_LIMIT = 56 * 1024 * 1024

_NT = (((1,), (1,)), ((), ()))


def _softplus(z):
    return jnp.maximum(z, 0.0) + jnp.log(1.0 + jnp.exp(-jnp.abs(z)))


def _rms(x):
    return x * lax.rsqrt(jnp.mean(x * x, axis=-1, keepdims=True) + RMS_EPS)


def _neg_suffix_matrix(n, extra):
    s = lax.broadcasted_iota(jnp.int32, (n, n + extra), 0)
    j = lax.broadcasted_iota(jnp.int32, (n, n + extra), 1)
    return jnp.where((s > j) | (j >= n), -1.0, 0.0).astype(BF16)


def _stick_block(z, negm, carry, mask, nk):
    sp = _softplus(z)
    spm = sp if mask is None else jnp.where(mask, sp, 0.0)
    hi = spm.astype(BF16)
    lo = (spm - hi.astype(F32)).astype(BF16)
    r = jnp.dot(hi, negm, preferred_element_type=F32) + jnp.dot(lo, negm, preferred_element_type=F32)
    later = r[:, :nk] + jnp.concatenate([carry] * (nk // LANES), axis=1)
    a = jnp.exp(z - sp + later)
    if mask is not None:
        a = jnp.where(mask, a, 0.0)
    return a, carry + r[:, nk:]


def _mod_kernel(c_ref, w_ref, b_ref, o_ref):
    c = c_ref[...]
    s = (c * (1.0 / (1.0 + jnp.exp(-c)))).astype(BF16)
    o_ref[...] = jnp.dot(s, w_ref[...].astype(BF16), preferred_element_type=F32) + b_ref[...]


def _modulation(c_all, w_ada, b_ada):
    n, d = c_all.shape
    n_out = w_ada.shape[1]
    return pl.pallas_call(
        _mod_kernel,
        grid=(n_out // d,),
        in_specs=[pl.BlockSpec((n, d), lambda j: (0, 0)),
                  pl.BlockSpec((d, d), lambda j: (0, j)),
                  pl.BlockSpec((1, d), lambda j: (0, j))],
        out_specs=pl.BlockSpec((n, d), lambda j: (0, j)),
        out_shape=jax.ShapeDtypeStruct((n, n_out), F32),
        name="mod",
    )(c_all, w_ada, b_ada)


def _norm_mod(x, g, sc, sh):
    return (_rms(x) * g) * (1.0 + sc) + sh


def _proj_prompt_kernel(x_ref, sh_ref, sc_ref, g_ref, wq_ref, wkvt_ref, wu_ref, q_ref, kt_ref, vt_ref, u_ref,
                        *, scale):
    hb = _norm_mod(x_ref[0], g_ref[...], sc_ref[0], sh_ref[0]).astype(BF16)
    q = jnp.dot(hb, wq_ref[...], preferred_element_type=F32)
    q_ref[0] = (q * scale).astype(BF16)
    u_ref[0] = jnp.dot(hb, wu_ref[...], preferred_element_type=F32)
    kvt = lax.dot_general(wkvt_ref[...], hb, _NT, preferred_element_type=F32)
    a = kt_ref.shape[1]
    kt_ref[0] = kvt[:a]
    vt_ref[0] = kvt[a:]


def _proj_prompt(x, mod, g, wq, wkvt, wu, tt, scale):
    n, t, d = x.shape
    a = wq.shape[1]
    const = lambda b, i: (0, 0)
    return pl.pallas_call(
        functools.partial(_proj_prompt_kernel, scale=scale),
        grid=(n, t // tt),
        in_specs=[pl.BlockSpec((1, tt, d), lambda b, i: (b, i, 0)),
                  pl.BlockSpec((1, 1, d), lambda b, i: (b, 0, 0)),
                  pl.BlockSpec((1, 1, d), lambda b, i: (b, 0, 1)),
                  pl.BlockSpec((1, d), const),
                  pl.BlockSpec(wq.shape, const),
                  pl.BlockSpec(wkvt.shape, const),
                  pl.BlockSpec(wu.shape, const)],
        out_specs=[pl.BlockSpec((1, tt, a), lambda b, i: (b, i, 0)),
                   pl.BlockSpec((1, a, tt), lambda b, i: (b, 0, i)),
                   pl.BlockSpec((1, a, tt), lambda b, i: (b, 0, i)),
                   pl.BlockSpec((1, tt, wu.shape[1]), lambda b, i: (b, i, 0))],
        out_shape=[jax.ShapeDtypeStruct((n, t, a), BF16),
                   jax.ShapeDtypeStruct((n, a, t), F32),
                   jax.ShapeDtypeStruct((n, a, t), F32),
                   jax.ShapeDtypeStruct((n, t, wu.shape[1]), F32)],
        compiler_params=pltpu.CompilerParams(dimension_semantics=("arbitrary", "arbitrary"),
                                             vmem_limit_bytes=VMEM_LIMIT),
        name="proj_prompt",
    )(x, mod, mod, g, wq, wkvt, wu)


def _proj_sample_kernel(x_ref, sh_ref, sc_ref, g_ref, w_ref, q_ref, k_ref, v_ref, u_ref, *, scale):
    nb, tt, d = x_ref.shape
    a = q_ref.shape[2]
    h = _norm_mod(x_ref[...], g_ref[...].reshape(1, 1, d), sc_ref[...], sh_ref[...])
    hb = h.reshape(nb * tt, d).astype(BF16)
    p = jnp.dot(hb, w_ref[...], preferred_element_type=F32)
    q_ref[...] = (p[:, :a] * scale).reshape(nb, tt, a)
    k_ref[...] = p[:, a:2 * a].reshape(nb, tt, a)
    v_ref[...] = p[:, 2 * a:3 * a].reshape(nb, tt, a)
    u_ref[...] = p[:, 3 * a:].reshape(nb, tt, u_ref.shape[2])


def _proj_sample(x, mod, g, w, nb, scale, a):
    n, t, d = x.shape
    pw = w.shape[1] - 3 * a
    blk = lambda width: pl.BlockSpec((nb, t, width), lambda b: (b, 0, 0))
    return pl.pallas_call(
        functools.partial(_proj_sample_kernel, scale=scale),
        grid=(n // nb,),
        in_specs=[blk(d),
                  pl.BlockSpec((nb, 1, d), lambda b: (b, 0, 0)),
                  pl.BlockSpec((nb, 1, d), lambda b: (b, 0, 1)),
                  pl.BlockSpec((1, d), lambda b: (0, 0)),
                  pl.BlockSpec(w.shape, lambda b: (0, 0))],
        out_specs=[blk(a), blk(a), blk(a), blk(pw)],
        out_shape=[jax.ShapeDtypeStruct((n, t, a), F32)] * 3 + [jax.ShapeDtypeStruct((n, t, pw), F32)],
        compiler_params=pltpu.CompilerParams(dimension_semantics=("arbitrary",), vmem_limit_bytes=VMEM_LIMIT),
        name="proj_sample",
    )(x, mod, mod, g, w)


def _attn_prompt_kernel(bias_ref, q_ref, kt_ref, vt_ref, o_ref, *, blk):
    qi = pl.program_id(1)
    n_heads = kt_ref.shape[1]
    negm = _neg_suffix_matrix(blk, LANES)
    row = lax.broadcasted_iota(jnp.int32, (blk, blk), 0)
    col = lax.broadcasted_iota(jnp.int32, (blk, blk), 1)
    diag_mask = col < row

    for h in range(n_heads):
        q_h = q_ref[0, :, h * HEAD_DIM:(h + 1) * HEAD_DIM]
        bias = bias_ref[h]

        def key_block(j, carry, acc, mask, h=h, q_h=q_h, bias=bias):
            start = pl.multiple_of(j * blk, blk)
            kt = kt_ref[0, h, :, pl.ds(start, blk)].astype(BF16)
            vt = vt_ref[0, h, :, pl.ds(start, blk)].astype(BF16)
            z = jnp.dot(q_h, kt, preferred_element_type=F32) + bias
            a, carry = _stick_block(z, negm, carry, mask, blk)
            acc = acc + lax.dot_general(a.astype(BF16), vt, _NT, preferred_element_type=F32)
            return carry, acc

        carry = jnp.zeros((blk, LANES), F32)
        acc = jnp.zeros((blk, HEAD_DIM), F32)
        carry, acc = key_block(qi, carry, acc, diag_mask)
        carry, acc = lax.fori_loop(0, qi, lambda jj, c: key_block(qi - 1 - jj, c[0], c[1], None), (carry, acc))
        o_ref[0, :, h * HEAD_DIM:(h + 1) * HEAD_DIM] = acc.astype(o_ref.dtype)


def _attn_prompt(bias, q, kt4, vt4, blk):
    n, t, a = q.shape
    return pl.pallas_call(
        functools.partial(_attn_prompt_kernel, blk=blk),
        grid=(n, t // blk),
        in_specs=[pl.BlockSpec(memory_space=pltpu.SMEM),
                  pl.BlockSpec((1, blk, a), lambda b, i: (b, i, 0)),
                  pl.BlockSpec((1,) + kt4.shape[1:], lambda b, i: (b, 0, 0, 0)),
                  pl.BlockSpec((1,) + vt4.shape[1:], lambda b, i: (b, 0, 0, 0))],
        out_specs=pl.BlockSpec((1, blk, a), lambda b, i: (b, i, 0)),
        out_shape=jax.ShapeDtypeStruct((n, t, a), BF16),
        compiler_params=pltpu.CompilerParams(dimension_semantics=("arbitrary", "arbitrary"),
                                             vmem_limit_bytes=VMEM_LIMIT),
        name="attn_prompt",
    )(bias, q, kt4, vt4)


def _decode_kernel(pt_ref, bias_ref, q_ref, kn_ref, vn_ref, kc_ref, vc_ref, o_ref, kbuf, vbuf, sem,
                   *, n_seq, n_pages, ch):
    s = pl.program_id(0)
    n_tok, width = q_ref.shape[1], q_ref.shape[2]
    n_heads = width // HEAD_DIM
    rows = n_heads * n_tok
    page = kbuf.shape[-1]
    nc = n_pages // ch
    total = n_seq * nc

    def chunk_copies(g, slot):
        seq = g // nc
        c = g % nc
        copies = []
        for i in range(ch):
            p = pt_ref[seq * n_pages + (n_pages - 1 - (c * ch + i))]
            copies.append(pltpu.make_async_copy(kc_ref.at[p], kbuf.at[slot, i], sem.at[0, slot]))
            copies.append(pltpu.make_async_copy(vc_ref.at[p], vbuf.at[slot, i], sem.at[1, slot]))
        return copies

    @pl.when(s == 0)
    def _():
        for cp in chunk_copies(0, 0):
            cp.start()

    r_i = lax.broadcasted_iota(jnp.int32, (rows, width), 0)
    c_i = lax.broadcasted_iota(jnp.int32, (rows, width), 1)
    head_diag = (r_i // n_tok) == (c_i // HEAD_DIM)
    q_bd = jnp.where(head_diag, jnp.concatenate([q_ref[0]] * n_heads, axis=0), 0.0).astype(BF16)

    r_p = lax.broadcasted_iota(jnp.int32, (rows, page), 0)
    l_p = lax.broadcasted_iota(jnp.int32, (rows, page), 1)
    bias = jnp.zeros((rows, page), F32)
    for h in range(n_heads):
        bias = jnp.where(r_p // n_tok == h, bias_ref[h], bias)
    negm = _neg_suffix_matrix(page, LANES)

    pad = jnp.zeros((page - n_tok, width), F32)
    kn = jnp.concatenate([kn_ref[0], pad], axis=0).astype(BF16)
    vn = jnp.concatenate([vn_ref[0], pad], axis=0).astype(BF16)
    z = lax.dot_general(q_bd, kn, _NT, preferred_element_type=F32) + bias
    a, carry = _stick_block(z, negm, jnp.zeros((rows, LANES), F32), l_p < (r_p % n_tok), page)
    acc = jnp.dot(a.astype(BF16), vn, preferred_element_type=F32)

    def chunk_body(c, state):
        g = s * nc + c
        slot = g % 2

        @pl.when(g + 1 < total)
        def _():
            for cp in chunk_copies(g + 1, 1 - slot):
                cp.start()

        for cp in chunk_copies(g, slot):
            cp.wait()

        def page_body(i, st):
            carry, acc = st
            kt = kbuf[slot, i].reshape(width, page).astype(BF16)
            vt = vbuf[slot, i].reshape(width, page).astype(BF16)
            z = jnp.dot(q_bd, kt, preferred_element_type=F32) + bias
            a, carry = _stick_block(z, negm, carry, None, page)
            acc = acc + lax.dot_general(a.astype(BF16), vt, _NT, preferred_element_type=F32)
            return carry, acc

        return lax.fori_loop(0, ch, page_body, state)

    carry, acc = lax.fori_loop(0, nc, chunk_body, (carry, acc))
    out = jnp.where(head_diag, acc, 0.0).reshape(n_heads, n_tok, width).sum(axis=0)
    o_ref[0] = out


def _decode_attn(page_table, bias, q, k_new, v_new, kc, vc, ch):
    n, t, a = q.shape
    n_pages = page_table.shape[1]
    page_shape = kc.shape[1:]
    tok = pl.BlockSpec((1, t, a), lambda s, pt: (s, 0, 0))
    return pl.pallas_call(
        functools.partial(_decode_kernel, n_seq=n, n_pages=n_pages, ch=ch),
        grid_spec=pltpu.PrefetchScalarGridSpec(
            num_scalar_prefetch=1,
            grid=(n,),
            in_specs=[pl.BlockSpec(memory_space=pltpu.SMEM), tok, tok, tok,
                      pl.BlockSpec(memory_space=pl.ANY), pl.BlockSpec(memory_space=pl.ANY)],
            out_specs=tok,
            scratch_shapes=[pltpu.VMEM((2, ch) + page_shape, F32),
                            pltpu.VMEM((2, ch) + page_shape, F32),
                            pltpu.SemaphoreType.DMA((2, 2))]),
        out_shape=jax.ShapeDtypeStruct((n, t, a), F32),
        compiler_params=pltpu.CompilerParams(dimension_semantics=("arbitrary",), vmem_limit_bytes=VMEM_LIMIT),
        name="decode_attn",
    )(page_table.reshape(-1), bias, q, k_new, v_new, kc, vc)


def _post_kernel(x_ref, attn_ref, u_ref, prev_ref, g1_ref, sh2_ref, sc2_ref, g2_ref, n2g_ref, fg_ref,
                 wpool_ref, pscale_ref, wo_ref, w1_ref, w2_ref, y_ref, ext_ref, *, pos0, zero_first_prev, ff_chunk,
                 final_norm):
    nb, tt, d = x_ref.shape
    m = nb * tt
    t_idx = pl.program_id(1)
    gw = wpool_ref.shape[1]

    u = u_ref[...]
    prev = prev_ref[...]
    if zero_first_prev:
        prev = jnp.where(t_idx > 0, prev, 0.0)
    ext_ref[:, 0:POOL_HALO, :] = prev
    ext_ref[:, POOL_HALO:POOL_HALO + tt, :] = u
    pos = pos0 + t_idx * tt + lax.broadcasted_iota(jnp.int32, (1, tt, gw), 1)
    parts = [attn_ref[...].reshape(m, attn_ref.shape[2]).astype(BF16)]
    for g, w in enumerate(POOL_WINDOWS):
        lo = g * gw
        ssum = ext_ref[:, POOL_HALO:POOL_HALO + tt, lo:lo + gw]
        for back in range(1, w):
            ssum = ssum + ext_ref[:, POOL_HALO - back:POOL_HALO - back + tt, lo:lo + gw]
        cnt = jnp.minimum(pos + 1, w).astype(F32)
        delta = (ssum / cnt - u[:, :, lo:lo + gw]).reshape(m, gw).astype(BF16)
        pooled = jnp.dot(delta, wpool_ref[g], preferred_element_type=F32) * pscale_ref[:, lo:lo + gw]
        parts.append(pooled.astype(BF16))
    mix = jnp.dot(jnp.concatenate(parts, axis=-1), wo_ref[...], preferred_element_type=F32)
    x1 = x_ref[...] + g1_ref[...] * mix.reshape(nb, tt, d)

    h2 = _norm_mod(x1, n2g_ref[...].reshape(1, 1, d), sc2_ref[...], sh2_ref[...]).reshape(m, d).astype(BF16)
    acc = jnp.zeros((m, d), F32)
    for c in range(w1_ref.shape[1] // ff_chunk):
        f = jnp.dot(h2, w1_ref[:, c * ff_chunk:(c + 1) * ff_chunk], preferred_element_type=F32)
        f = jnp.square(jnp.maximum(f, 0.0)).astype(BF16)
        acc = acc + jnp.dot(f, w2_ref[c * ff_chunk:(c + 1) * ff_chunk, :], preferred_element_type=F32)
    x2 = x1 + g2_ref[...] * acc.reshape(nb, tt, d)
    y_ref[...] = _rms(x2) * fg_ref[...].reshape(1, 1, d) if final_norm else x2


def _post(x, attn, u, prev, prev_map, mod, n2g, fg, wpool, pscale, wo, w1, w2, nb, tt, pos0, zero_first_prev,
          final_norm):
    n, t, d = x.shape
    a = attn.shape[2]
    pw = u.shape[2]
    tile = lambda width: pl.BlockSpec((nb, tt, width), lambda b, i: (b, i, 0))
    modspec = lambda k: pl.BlockSpec((nb, 1, d), lambda b, i: (b, 0, k))
    const2 = lambda b, i: (0, 0)
    resident = lambda arr: pl.BlockSpec(arr.shape, (lambda b, i: (0,) * arr.ndim), pipeline_mode=pl.Buffered(1))
    return pl.pallas_call(
        functools.partial(_post_kernel, pos0=pos0, zero_first_prev=zero_first_prev, ff_chunk=1024,
                          final_norm=final_norm),
        grid=(n // nb, t // tt),
        in_specs=[tile(d), tile(a), tile(pw),
                  pl.BlockSpec((nb, POOL_HALO, pw), prev_map),
                  modspec(2), modspec(3), modspec(4), modspec(5),
                  pl.BlockSpec((1, d), const2), pl.BlockSpec((1, d), const2),
                  resident(wpool), pl.BlockSpec((1, pw), const2),
                  resident(wo), resident(w1), resident(w2)],
        out_specs=tile(d),
        out_shape=jax.ShapeDtypeStruct((n, t, d), F32),
        scratch_shapes=[pltpu.VMEM((nb, POOL_HALO + tt, pw), F32)],
        compiler_params=pltpu.CompilerParams(dimension_semantics=("arbitrary", "arbitrary"),
                                             vmem_limit_bytes=VMEM_LIMIT),
        name="post",
    )(x, attn, u, prev, mod, mod, mod, mod, n2g, fg, wpool, pscale, wo, w1, w2)


def kernel(x_prompt, x_sample, c_prompt, c_sample, cache_k, cache_v, state_pool, page_table, norm1_g, norm2_g,
           w_ada, b_ada, w_in, sb_bias, w_pool, pool_scale, w_o, w_ff1, w_ff2, final_g):
    n_b, seq, d = x_prompt.shape
    n_d, dec_seq, _ = x_sample.shape
    depth = w_in.shape[0]
    n_heads = sb_bias.shape[1]
    a = n_heads * HEAD_DIM
    past_len = page_table.shape[1] * cache_k.shape[2]
    scale = HEAD_DIM ** -0.5
    fg = final_g.reshape(1, d)

    xp, xs = x_prompt, x_sample
    outs = [[] for _ in range(6)]
    for l in range(depth):
        mod = _modulation(jnp.concatenate([c_prompt, c_sample], axis=0), w_ada[l], b_ada[l].reshape(1, -1))
        mod_p = mod[:n_b].reshape(n_b, 1, -1)
        mod_s = mod[n_b:].reshape(n_d, 1, -1)
        g1 = norm1_g[l].reshape(1, d)
        g2 = norm2_g[l].reshape(1, d)
        w_in_b = w_in[l].astype(BF16)
        wq, wu = w_in_b[:, :a], w_in_b[:, 3 * a:]
        wkvt = w_in_b[:, a:3 * a].T
        wpool = w_pool[l].astype(BF16)
        pscale = pool_scale[l].reshape(1, -1)
        wo, w1, w2 = w_o[l].astype(BF16), w_ff1[l].astype(BF16), w_ff2[l].astype(BF16)
        bias = sb_bias[l]

        tt = 512
        q_p, kt_p, vt_p, u_p = _proj_prompt(xp, mod_p, g1, wq, wkvt, wu, tt, scale)
        kt4 = kt_p.reshape(n_b, n_heads, HEAD_DIM, seq)
        vt4 = vt_p.reshape(n_b, n_heads, HEAD_DIM, seq)
        attn_p = _attn_prompt(bias, q_p, kt4, vt4, 256)
        halo_per_tile = tt // POOL_HALO
        xp = _post(xp, attn_p, u_p, u_p, lambda b, i: (b, jnp.maximum(i * halo_per_tile - 1, 0), 0),
                   mod_p, g2, fg, wpool, pscale, wo, w1, w2, 1, tt, 0, True, l == depth - 1)
        outs[0].append(jnp.transpose(kt4, (0, 3, 1, 2)))
        outs[1].append(jnp.transpose(vt4, (0, 3, 1, 2)))
        outs[2].append(u_p[:, seq - (POOL_HALO - 1):])

        q_s, k_s, v_s, u_s = _proj_sample(xs, mod_s, g1, w_in_b, 64, scale, a)
        kc = jnp.transpose(cache_k[l], (0, 2, 3, 1))
        vc = jnp.transpose(cache_v[l], (0, 2, 3, 1))
        attn_s = _decode_attn(page_table, bias, q_s, k_s, v_s, kc, vc, 8)
        prev_s = jnp.concatenate([jnp.zeros((n_d, 1, u_s.shape[2]), F32), state_pool[l]], axis=1)
        xs = _post(xs, attn_s, u_s, prev_s, lambda b, i: (b, 0, 0),
                   mod_s, g2, fg, wpool, pscale, wo, w1, w2, 64, dec_seq, past_len, False, l == depth - 1)
        outs[3].append(k_s.reshape(n_d, dec_seq, n_heads, HEAD_DIM))
        outs[4].append(v_s.reshape(n_d, dec_seq, n_heads, HEAD_DIM))
        outs[5].append(jnp.concatenate([state_pool[l], u_s], axis=1)[:, -(POOL_HALO - 1):])

    stk = [jnp.stack(o) for o in outs]
    return (xp, xs, stk[0], stk[1], stk[2], stk[3], stk[4], stk[5])
```

```python
import functools

import jax
import jax.numpy as jnp
from jax import lax
from jax.experimental import pallas as pl
from jax.experimental.pallas import tpu as pltpu

F32 = jnp.float32
BF16 = jnp.bfloat16

HEAD_DIM = 64
POOL_WINDOWS = (2, 4, 8, 16)
POOL_HALO = 16
RMS_EPS = 1e-6
LANES = 128
VMEM_LIMIT = 56 * 1024 * 1024

_NT = (((1,), (1,)), ((), ()))


def _softplus(z):
    return jnp.maximum(z, 0.0) + jnp.log(1.0 + jnp.exp(-jnp.abs(z)))


def _rms(x):
    return x * lax.rsqrt(jnp.mean(x * x, axis=-1, keepdims=True) + RMS_EPS)


def _neg_suffix_matrix(n, extra):
    s = lax.broadcasted_iota(jnp.int32, (n, n + extra), 0)
    j = lax.broadcasted_iota(jnp.int32, (n, n + extra), 1)
    return jnp.where((s > j) | (j >= n), -1.0, 0.0).astype(BF16)


def _suffix_sums(sps, negm, two_pass):
    his = [sp.astype(BF16) for sp in sps]
    rs = [jnp.dot(hi, negm, preferred_element_type=F32) for hi in his]
    if two_pass:
        los = [(sp - hi.astype(F32)).astype(BF16) for sp, hi in zip(sps, his)]
        rs = [r + jnp.dot(lo, negm, preferred_element_type=F32) for r, lo in zip(rs, los)]
    return rs


def _stick_weights(z, sp, r, carry, mask, nk):
    later = r[:, :nk]
    if carry is not None:
        later = later + jnp.concatenate([carry] * (nk // LANES), axis=1)
    a = jnp.exp(z - sp + later)
    if mask is not None:
        a = jnp.where(mask, a, 0.0)
    return a.astype(BF16)


def _mod_kernel(c_ref, w_ref, b_ref, o_ref):
    c = c_ref[...]
    s = (c * (1.0 / (1.0 + jnp.exp(-c)))).astype(BF16)
    o_ref[...] = jnp.dot(s, w_ref[...].astype(BF16), preferred_element_type=F32) + b_ref[...]


def _modulation(c_all, w_ada, b_ada):
    n, d = c_all.shape
    n_out = w_ada.shape[1]
    return pl.pallas_call(
        _mod_kernel,
        grid=(n_out // d,),
        in_specs=[pl.BlockSpec((n, d), lambda j: (0, 0)),
                  pl.BlockSpec((d, d), lambda j: (0, j)),
                  pl.BlockSpec((1, d), lambda j: (0, j))],
        out_specs=pl.BlockSpec((n, d), lambda j: (0, j)),
        out_shape=jax.ShapeDtypeStruct((n, n_out), F32),
        name="mod",
    )(c_all, w_ada, b_ada)


def _norm_mod(x, g, sc, sh):
    return (_rms(x) * g) * (1.0 + sc) + sh


def _proj_prompt_kernel(x_ref, sh_ref, sc_ref, g_ref, wq_ref, wkvt_ref, wu_ref, q_ref, kt_ref, vt_ref, u_ref,
                        *, scale):
    hb = _norm_mod(x_ref[0], g_ref[...], sc_ref[0], sh_ref[0]).astype(BF16)
    q = jnp.dot(hb, wq_ref[...], preferred_element_type=F32)
    q_ref[0] = (q * scale).astype(BF16)
    u_ref[0] = jnp.dot(hb, wu_ref[...], preferred_element_type=F32)
    kvt = lax.dot_general(wkvt_ref[...], hb, _NT, preferred_element_type=F32)
    a = kt_ref.shape[1]
    kt_ref[0] = kvt[:a]
    vt_ref[0] = kvt[a:]


def _proj_prompt(x, mod, g, wq, wkvt, wu, tt, scale):
    n, t, d = x.shape
    a = wq.shape[1]
    const = lambda b, i: (0, 0)
    return pl.pallas_call(
        functools.partial(_proj_prompt_kernel, scale=scale),
        grid=(n, t // tt),
        in_specs=[pl.BlockSpec((1, tt, d), lambda b, i: (b, i, 0)),
                  pl.BlockSpec((1, 1, d), lambda b, i: (b, 0, 0)),
                  pl.BlockSpec((1, 1, d), lambda b, i: (b, 0, 1)),
                  pl.BlockSpec((1, d), const),
                  pl.BlockSpec(wq.shape, const),
                  pl.BlockSpec(wkvt.shape, const),
                  pl.BlockSpec(wu.shape, const)],
        out_specs=[pl.BlockSpec((1, tt, a), lambda b, i: (b, i, 0)),
                   pl.BlockSpec((1, a, tt), lambda b, i: (b, 0, i)),
                   pl.BlockSpec((1, a, tt), lambda b, i: (b, 0, i)),
                   pl.BlockSpec((1, tt, wu.shape[1]), lambda b, i: (b, i, 0))],
        out_shape=[jax.ShapeDtypeStruct((n, t, a), BF16),
                   jax.ShapeDtypeStruct((n, a, t), F32),
                   jax.ShapeDtypeStruct((n, a, t), F32),
                   jax.ShapeDtypeStruct((n, t, wu.shape[1]), F32)],
        compiler_params=pltpu.CompilerParams(dimension_semantics=("arbitrary", "arbitrary"),
                                             vmem_limit_bytes=VMEM_LIMIT),
        name="proj_prompt",
    )(x, mod, mod, g, wq, wkvt, wu)


def _proj_sample_kernel(x_ref, sh_ref, sc_ref, g_ref, w_ref, q_ref, k_ref, v_ref, u_ref, *, scale):
    nb, tt, d = x_ref.shape
    a = q_ref.shape[2]
    h = _norm_mod(x_ref[...], g_ref[...].reshape(1, 1, d), sc_ref[...], sh_ref[...])
    hb = h.reshape(nb * tt, d).astype(BF16)
    p = jnp.dot(hb, w_ref[...], preferred_element_type=F32)
    q_ref[...] = (p[:, :a] * scale).reshape(nb, tt, a)
    k_ref[...] = p[:, a:2 * a].reshape(nb, tt, a)
    v_ref[...] = p[:, 2 * a:3 * a].reshape(nb, tt, a)
    u_ref[...] = p[:, 3 * a:].reshape(nb, tt, u_ref.shape[2])


def _proj_sample(x, mod, g, w, nb, scale, a):
    n, t, d = x.shape
    pw = w.shape[1] - 3 * a
    blk = lambda width: pl.BlockSpec((nb, t, width), lambda b: (b, 0, 0))
    return pl.pallas_call(
        functools.partial(_proj_sample_kernel, scale=scale),
        grid=(n // nb,),
        in_specs=[blk(d),
                  pl.BlockSpec((nb, 1, d), lambda b: (b, 0, 0)),
                  pl.BlockSpec((nb, 1, d), lambda b: (b, 0, 1)),
                  pl.BlockSpec((1, d), lambda b: (0, 0)),
                  pl.BlockSpec(w.shape, lambda b: (0, 0))],
        out_specs=[blk(a), blk(a), blk(a), blk(pw)],
        out_shape=[jax.ShapeDtypeStruct((n, t, a), F32)] * 3 + [jax.ShapeDtypeStruct((n, t, pw), F32)],
        compiler_params=pltpu.CompilerParams(dimension_semantics=("arbitrary",), vmem_limit_bytes=VMEM_LIMIT),
        name="proj_sample",
    )(x, mod, mod, g, w)


def _attn_prompt_kernel(bias_ref, q_ref, kt_ref, vt_ref, o_ref, qh_ref, carry_ref, acc_ref, *, blk, two_pass):
    qi = pl.program_id(1)
    n_heads = kt_ref.shape[1]
    negm = _neg_suffix_matrix(blk, LANES)
    row = lax.broadcasted_iota(jnp.int32, (blk, blk), 0)
    col = lax.broadcasted_iota(jnp.int32, (blk, blk), 1)
    diag_mask = col < row

    for h in range(n_heads):
        qh_ref[h] = q_ref[0, :, h * HEAD_DIM:(h + 1) * HEAD_DIM]

    def key_block(j, mask, first):
        start = pl.multiple_of(j * blk, blk)
        heads = range(n_heads)
        zs = [jnp.dot(qh_ref[h], kt_ref[0, h, :, pl.ds(start, blk)].astype(BF16), preferred_element_type=F32)
              + bias_ref[h] for h in heads]
        sps = [_softplus(z) for z in zs]
        rs = _suffix_sums(sps if mask is None else [jnp.where(mask, sp, 0.0) for sp in sps], negm, two_pass)
        aa = []
        for h in heads:
            carry = None if first else carry_ref[h]
            aa.append(_stick_weights(zs[h], sps[h], rs[h], carry, mask, blk))
            carry_ref[h] = rs[h][:, blk:] if first else carry + rs[h][:, blk:]
        for h in heads:
            vt = vt_ref[0, h, :, pl.ds(start, blk)].astype(BF16)
            pv = lax.dot_general(aa[h], vt, _NT, preferred_element_type=F32)
            acc_ref[h] = pv if first else acc_ref[h] + pv

    key_block(qi, diag_mask, True)

    def older_block(jj, _):
        key_block(qi - 1 - jj, None, False)
        return 0

    lax.fori_loop(0, qi, older_block, 0)
    o_ref[0] = jnp.concatenate([acc_ref[h] for h in range(n_heads)], axis=1).astype(o_ref.dtype)


def _attn_prompt(bias, q, kt4, vt4, blk):
    n, t, a = q.shape
    n_heads = kt4.shape[1]
    return pl.pallas_call(
        functools.partial(_attn_prompt_kernel, blk=blk, two_pass=False),
        grid=(n, t // blk),
        in_specs=[pl.BlockSpec(memory_space=pltpu.SMEM),
                  pl.BlockSpec((1, blk, a), lambda b, i: (b, i, 0)),
                  pl.BlockSpec((1,) + kt4.shape[1:], lambda b, i: (b, 0, 0, 0)),
                  pl.BlockSpec((1,) + vt4.shape[1:], lambda b, i: (b, 0, 0, 0))],
        out_specs=pl.BlockSpec((1, blk, a), lambda b, i: (b, i, 0)),
        out_shape=jax.ShapeDtypeStruct((n, t, a), BF16),
        scratch_shapes=[pltpu.VMEM((n_heads, blk, HEAD_DIM), BF16),
                        pltpu.VMEM((n_heads, blk, LANES), F32),
                        pltpu.VMEM((n_heads, blk, HEAD_DIM), F32)],
        compiler_params=pltpu.CompilerParams(dimension_semantics=("arbitrary", "arbitrary"),
                                             vmem_limit_bytes=VMEM_LIMIT),
        name="attn_prompt",
    )(bias, q, kt4, vt4)


def _decode_kernel(pt_ref, bias_ref, q_ref, kn_ref, vn_ref, kc_ref, vc_ref, o_ref, kbuf, vbuf, sem,
                   *, n_seq, n_pages, ch):
    s = pl.program_id(0)
    n_tok, width = q_ref.shape[1], q_ref.shape[2]
    n_heads = width // HEAD_DIM
    rows = n_heads * n_tok
    page = kbuf.shape[-1]
    nc = n_pages // ch
    total = n_seq * nc

    def chunk_copies(g, slot):
        seq = g // nc
        c = g % nc
        copies = []
        for i in range(ch):
            p = pt_ref[seq * n_pages + (n_pages - 1 - (c * ch + i))]
            copies.append(pltpu.make_async_copy(kc_ref.at[p], kbuf.at[slot, i], sem.at[0, slot]))
            copies.append(pltpu.make_async_copy(vc_ref.at[p], vbuf.at[slot, i], sem.at[1, slot]))
        return copies

    @pl.when(s == 0)
    def _():
        for cp in chunk_copies(0, 0):
            cp.start()

    r_i = lax.broadcasted_iota(jnp.int32, (rows, width), 0)
    c_i = lax.broadcasted_iota(jnp.int32, (rows, width), 1)
    head_diag = (r_i // n_tok) == (c_i // HEAD_DIM)
    q_bd = jnp.where(head_diag, jnp.concatenate([q_ref[0]] * n_heads, axis=0), 0.0).astype(BF16)

    r_p = lax.broadcasted_iota(jnp.int32, (rows, page), 0)
    l_p = lax.broadcasted_iota(jnp.int32, (rows, page), 1)
    bias = jnp.zeros((rows, page), F32)
    for h in range(n_heads):
        bias = jnp.where(r_p // n_tok == h, bias_ref[h], bias)
    negm = _neg_suffix_matrix(page, LANES)

    pad = jnp.zeros((page - n_tok, width), F32)
    kn = jnp.concatenate([kn_ref[0], pad], axis=0).astype(BF16)
    vn = jnp.concatenate([vn_ref[0], pad], axis=0).astype(BF16)
    new_mask = l_p < (r_p % n_tok)
    z = lax.dot_general(q_bd, kn, _NT, preferred_element_type=F32) + bias
    sp = _softplus(z)
    r = _suffix_sums([jnp.where(new_mask, sp, 0.0)], negm, True)[0]
    carry = r[:, page:]
    acc = jnp.dot(_stick_weights(z, sp, r, None, new_mask, page), vn, preferred_element_type=F32)

    def chunk_body(c, state):
        g = s * nc + c
        slot = g % 2

        @pl.when(g + 1 < total)
        def _():
            for cp in chunk_copies(g + 1, 1 - slot):
                cp.start()

        for cp in chunk_copies(g, slot):
            cp.wait()

        carry, acc = state
        pages = range(ch)
        zs = [jnp.dot(q_bd, kbuf[slot, i].reshape(width, page).astype(BF16), preferred_element_type=F32) + bias
              for i in pages]
        sps = [_softplus(z) for z in zs]
        rs = _suffix_sums(sps, negm, True)
        aa = []
        for i in pages:
            aa.append(_stick_weights(zs[i], sps[i], rs[i], carry, None, page))
            carry = carry + rs[i][:, page:]
        for i in pages:
            vt = vbuf[slot, i].reshape(width, page).astype(BF16)
            acc = acc + lax.dot_general(aa[i], vt, _NT, preferred_element_type=F32)
        return carry, acc

    carry, acc = lax.fori_loop(0, nc, chunk_body, (carry, acc))
    out = jnp.where(head_diag, acc, 0.0).reshape(n_heads, n_tok, width).sum(axis=0)
    o_ref[0] = out


def _decode_attn(page_table, bias, q, k_new, v_new, kc, vc, ch):
    n, t, a = q.shape
    n_pages = page_table.shape[1]
    page_shape = kc.shape[1:]
    tok = pl.BlockSpec((1, t, a), lambda s, pt: (s, 0, 0))
    return pl.pallas_call(
        functools.partial(_decode_kernel, n_seq=n, n_pages=n_pages, ch=ch),
        grid_spec=pltpu.PrefetchScalarGridSpec(
            num_scalar_prefetch=1,
            grid=(n,),
            in_specs=[pl.BlockSpec(memory_space=pltpu.SMEM), tok, tok, tok,
                      pl.BlockSpec(memory_space=pl.ANY), pl.BlockSpec(memory_space=pl.ANY)],
            out_specs=tok,
            scratch_shapes=[pltpu.VMEM((2, ch) + page_shape, F32),
                            pltpu.VMEM((2, ch) + page_shape, F32),
                            pltpu.SemaphoreType.DMA((2, 2))]),
        out_shape=jax.ShapeDtypeStruct((n, t, a), F32),
        compiler_params=pltpu.CompilerParams(dimension_semantics=("arbitrary",), vmem_limit_bytes=VMEM_LIMIT),
        name="decode_attn",
    )(page_table.reshape(-1), bias, q, k_new, v_new, kc, vc)


def _post_kernel(x_ref, attn_ref, u_ref, prev_ref, g1_ref, sh2_ref, sc2_ref, g2_ref, n2g_ref, fg_ref,
                 wpool_ref, pscale_ref, wo_ref, w1_ref, w2_ref, y_ref, ext_ref, *, pos0, zero_first_prev, ff_chunk,
                 final_norm):
    nb, tt, d = x_ref.shape
    m = nb * tt
    t_idx = pl.program_id(1)
    gw = wpool_ref.shape[1]

    u = u_ref[...]
    prev = prev_ref[...]
    if zero_first_prev:
        prev = jnp.where(t_idx > 0, prev, 0.0)
    ext_ref[:, 0:POOL_HALO, :] = prev
    ext_ref[:, POOL_HALO:POOL_HALO + tt, :] = u
    pos = pos0 + t_idx * tt + lax.broadcasted_iota(jnp.int32, (1, tt, gw), 1)
    parts = [attn_ref[...].reshape(m, attn_ref.shape[2]).astype(BF16)]
    for g, w in enumerate(POOL_WINDOWS):
        lo = g * gw
        ssum = ext_ref[:, POOL_HALO:POOL_HALO + tt, lo:lo + gw]
        for back in range(1, w):
            ssum = ssum + ext_ref[:, POOL_HALO - back:POOL_HALO - back + tt, lo:lo + gw]
        cnt = jnp.minimum(pos + 1, w).astype(F32)
        delta = (ssum / cnt - u[:, :, lo:lo + gw]).reshape(m, gw).astype(BF16)
        pooled = jnp.dot(delta, wpool_ref[g], preferred_element_type=F32) * pscale_ref[:, lo:lo + gw]
        parts.append(pooled.astype(BF16))
    mix = jnp.dot(jnp.concatenate(parts, axis=-1), wo_ref[...], preferred_element_type=F32)
    x1 = x_ref[...] + g1_ref[...] * mix.reshape(nb, tt, d)

    h2 = _norm_mod(x1, n2g_ref[...].reshape(1, 1, d), sc2_ref[...], sh2_ref[...]).reshape(m, d).astype(BF16)
    acc = jnp.zeros((m, d), F32)
    for c in range(w1_ref.shape[1] // ff_chunk):
        f = jnp.dot(h2, w1_ref[:, c * ff_chunk:(c + 1) * ff_chunk], preferred_element_type=F32)
        f = jnp.square(jnp.maximum(f, 0.0)).astype(BF16)
        acc = acc + jnp.dot(f, w2_ref[c * ff_chunk:(c + 1) * ff_chunk, :], preferred_element_type=F32)
    x2 = x1 + g2_ref[...] * acc.reshape(nb, tt, d)
    y_ref[...] = _rms(x2) * fg_ref[...].reshape(1, 1, d) if final_norm else x2


def _post(x, attn, u, prev, prev_map, mod, n2g, fg, wpool, pscale, wo, w1, w2, nb, tt, pos0, zero_first_prev,
          final_norm):
    n, t, d = x.shape
    a = attn.shape[2]
    pw = u.shape[2]
    tile = lambda width: pl.BlockSpec((nb, tt, width), lambda b, i: (b, i, 0))
    modspec = lambda k: pl.BlockSpec((nb, 1, d), lambda b, i: (b, 0, k))
    const2 = lambda b, i: (0, 0)
    resident = lambda arr: pl.BlockSpec(arr.shape, (lambda b, i: (0,) * arr.ndim), pipeline_mode=pl.Buffered(1))
    return pl.pallas_call(
        functools.partial(_post_kernel, pos0=pos0, zero_first_prev=zero_first_prev, ff_chunk=1024,
                          final_norm=final_norm),
        grid=(n // nb, t // tt),
        in_specs=[tile(d), tile(a), tile(pw),
                  pl.BlockSpec((nb, POOL_HALO, pw), prev_map),
                  modspec(2), modspec(3), modspec(4), modspec(5),
                  pl.BlockSpec((1, d), const2), pl.BlockSpec((1, d), const2),
                  resident(wpool), pl.BlockSpec((1, pw), const2),
                  resident(wo), resident(w1), resident(w2)],
        out_specs=tile(d),
        out_shape=jax.ShapeDtypeStruct((n, t, d), F32),
        scratch_shapes=[pltpu.VMEM((nb, POOL_HALO + tt, pw), F32)],
        compiler_params=pltpu.CompilerParams(dimension_semantics=("arbitrary", "arbitrary"),
                                             vmem_limit_bytes=VMEM_LIMIT),
        name="post",
    )(x, attn, u, prev, mod, mod, mod, mod, n2g, fg, wpool, pscale, wo, w1, w2)


def kernel(x_prompt, x_sample, c_prompt, c_sample, cache_k, cache_v, state_pool, page_table, norm1_g, norm2_g,
           w_ada, b_ada, w_in, sb_bias, w_pool, pool_scale, w_o, w_ff1, w_ff2, final_g):
    n_b, seq, d = x_prompt.shape
    n_d, dec_seq, _ = x_sample.shape
    depth = w_in.shape[0]
    n_heads = sb_bias.shape[1]
    a = n_heads * HEAD_DIM
    past_len = page_table.shape[1] * cache_k.shape[2]
    scale = HEAD_DIM ** -0.5
    fg = final_g.reshape(1, d)

    xp, xs = x_prompt, x_sample
    outs = [[] for _ in range(6)]
    for l in range(depth):
        mod = _modulation(jnp.concatenate([c_prompt, c_sample], axis=0), w_ada[l], b_ada[l].reshape(1, -1))
        mod_p = mod[:n_b].reshape(n_b, 1, -1)
        mod_s = mod[n_b:].reshape(n_d, 1, -1)
        g1 = norm1_g[l].reshape(1, d)
        g2 = norm2_g[l].reshape(1, d)
        w_in_b = w_in[l].astype(BF16)
        wq, wu = w_in_b[:, :a], w_in_b[:, 3 * a:]
        wkvt = w_in_b[:, a:3 * a].T
        wpool = w_pool[l].astype(BF16)
        pscale = pool_scale[l].reshape(1, -1)
        wo, w1, w2 = w_o[l].astype(BF16), w_ff1[l].astype(BF16), w_ff2[l].astype(BF16)
        bias = sb_bias[l]

        tt = 512
        q_p, kt_p, vt_p, u_p = _proj_prompt(xp, mod_p, g1, wq, wkvt, wu, tt, scale)
        kt4 = kt_p.reshape(n_b, n_heads, HEAD_DIM, seq)
        vt4 = vt_p.reshape(n_b, n_heads, HEAD_DIM, seq)
        attn_p = _attn_prompt(bias, q_p, kt4, vt4, 256)
        halo_per_tile = tt // POOL_HALO
        xp = _post(xp, attn_p, u_p, u_p, lambda b, i: (b, jnp.maximum(i * halo_per_tile - 1, 0), 0),
                   mod_p, g2, fg, wpool, pscale, wo, w1, w2, 1, tt, 0, True, l == depth - 1)
        outs[0].append(jnp.transpose(kt4, (0, 3, 1, 2)))
        outs[1].append(jnp.transpose(vt4, (0, 3, 1, 2)))
        outs[2].append(u_p[:, seq - (POOL_HALO - 1):])

        q_s, k_s, v_s, u_s = _proj_sample(xs, mod_s, g1, w_in_b, 64, scale, a)
        kc = jnp.transpose(cache_k[l], (0, 2, 3, 1))
        vc = jnp.transpose(cache_v[l], (0, 2, 3, 1))
        attn_s = _decode_attn(page_table, bias, q_s, k_s, v_s, kc, vc, 8)
        prev_s = jnp.concatenate([jnp.zeros((n_d, 1, u_s.shape[2]), F32), state_pool[l]], axis=1)
        xs = _post(xs, attn_s, u_s, prev_s, lambda b, i: (b, 0, 0),
                   mod_s, g2, fg, wpool, pscale, wo, w1, w2, 64, dec_seq, past_len, False, l == depth - 1)
        outs[3].append(k_s.reshape(n_d, dec_seq, n_heads, HEAD_DIM))
        outs[4].append(v_s.reshape(n_d, dec_seq, n_heads, HEAD_DIM))
        outs[5].append(jnp.concatenate([state_pool[l], u_s], axis=1)[:, -(POOL_HALO - 1):])

    stk = [jnp.stack(o) for o in outs]
    return (xp, xs, stk[0], stk[1], stk[2], stk[3], stk[4], stk[5])
```

```python
import functools

import jax
import jax.numpy as jnp
from jax import lax
from jax.experimental import pallas as pl
from jax.experimental.pallas import tpu as pltpu

F32 = jnp.float32
BF16 = jnp.bfloat16

HEAD_DIM = 64
POOL_WINDOWS = (2, 4, 8, 16)
POOL_HALO = 16
RMS_EPS = 1e-6
LANES = 128
VMEM_LIMIT = 56 * 1024 * 1024

_NT = (((1,), (1,)), ((), ()))


def _softplus(z):
    return jnp.maximum(z, 0.0) + jnp.log(1.0 + jnp.exp(-jnp.abs(z)))


def _rms(x):
    return x * lax.rsqrt(jnp.mean(x * x, axis=-1, keepdims=True) + RMS_EPS)


def _neg_suffix_matrix(n, extra):
    s = lax.broadcasted_iota(jnp.int32, (n, n + extra), 0)
    j = lax.broadcasted_iota(jnp.int32, (n, n + extra), 1)
    return jnp.where((s > j) | (j >= n), -1.0, 0.0).astype(BF16)


def _suffix_sums(sps, negm, two_pass):
    his = [sp.astype(BF16) for sp in sps]
    rs = [jnp.dot(hi, negm, preferred_element_type=F32) for hi in his]
    if two_pass:
        los = [(sp - hi.astype(F32)).astype(BF16) for sp, hi in zip(sps, his)]
        rs = [r + jnp.dot(lo, negm, preferred_element_type=F32) for r, lo in zip(rs, los)]
    return rs


def _stick_weights(z, sp, r, carry, mask, nk):
    later = r[:, :nk]
    if carry is not None:
        later = later + jnp.concatenate([carry] * (nk // LANES), axis=1)
    a = jnp.exp(z - sp + later)
    if mask is not None:
        a = jnp.where(mask, a, 0.0)
    return a.astype(BF16)


def _mod_kernel(c_ref, w_ref, b_ref, o_ref):
    c = c_ref[...]
    s = (c * (1.0 / (1.0 + jnp.exp(-c)))).astype(BF16)
    o_ref[...] = jnp.dot(s, w_ref[...].astype(BF16), preferred_element_type=F32) + b_ref[...]


def _modulation(c_all, w_ada, b_ada):
    n, d = c_all.shape
    n_out = w_ada.shape[1]
    return pl.pallas_call(
        _mod_kernel,
        grid=(n_out // d,),
        in_specs=[pl.BlockSpec((n, d), lambda j: (0, 0)),
                  pl.BlockSpec((d, d), lambda j: (0, j)),
                  pl.BlockSpec((1, d), lambda j: (0, j))],
        out_specs=pl.BlockSpec((n, d), lambda j: (0, j)),
        out_shape=jax.ShapeDtypeStruct((n, n_out), F32),
        name="mod",
    )(c_all, w_ada, b_ada)


def _norm_mod(x, g, sc, sh):
    return (_rms(x) * g) * (1.0 + sc) + sh


def _proj_prompt_kernel(x_ref, sh_ref, sc_ref, g_ref, wq_ref, wkvt_ref, wu_ref, q_ref, kt_ref, vt_ref, u_ref,
                        *, scale):
    hb = _norm_mod(x_ref[0], g_ref[...], sc_ref[0], sh_ref[0]).astype(BF16)
    q = jnp.dot(hb, wq_ref[...], preferred_element_type=F32)
    q_ref[0] = (q * scale).astype(BF16)
    u_ref[0] = jnp.dot(hb, wu_ref[...], preferred_element_type=F32)
    kvt = lax.dot_general(wkvt_ref[...], hb, _NT, preferred_element_type=F32)
    a = kt_ref.shape[1]
    kt_ref[0] = kvt[:a]
    vt_ref[0] = kvt[a:]


def _proj_prompt(x, mod, g, wq, wkvt, wu, tt, scale):
    n, t, d = x.shape
    a = wq.shape[1]
    const = lambda b, i: (0, 0)
    return pl.pallas_call(
        functools.partial(_proj_prompt_kernel, scale=scale),
        grid=(n, t // tt),
        in_specs=[pl.BlockSpec((1, tt, d), lambda b, i: (b, i, 0)),
                  pl.BlockSpec((1, 1, d), lambda b, i: (b, 0, 0)),
                  pl.BlockSpec((1, 1, d), lambda b, i: (b, 0, 1)),
                  pl.BlockSpec((1, d), const),
                  pl.BlockSpec(wq.shape, const),
                  pl.BlockSpec(wkvt.shape, const),
                  pl.BlockSpec(wu.shape, const)],
        out_specs=[pl.BlockSpec((1, tt, a), lambda b, i: (b, i, 0)),
                   pl.BlockSpec((1, a, tt), lambda b, i: (b, 0, i)),
                   pl.BlockSpec((1, a, tt), lambda b, i: (b, 0, i)),
                   pl.BlockSpec((1, tt, wu.shape[1]), lambda b, i: (b, i, 0))],
        out_shape=[jax.ShapeDtypeStruct((n, t, a), BF16),
                   jax.ShapeDtypeStruct((n, a, t), F32),
                   jax.ShapeDtypeStruct((n, a, t), F32),
                   jax.ShapeDtypeStruct((n, t, wu.shape[1]), F32)],
        compiler_params=pltpu.CompilerParams(dimension_semantics=("arbitrary", "arbitrary"),
                                             vmem_limit_bytes=VMEM_LIMIT),
        name="proj_prompt",
    )(x, mod, mod, g, wq, wkvt, wu)


def _proj_sample_kernel(x_ref, sh_ref, sc_ref, g_ref, w_ref, q_ref, k_ref, v_ref, u_ref, *, scale):
    nb, tt, d = x_ref.shape
    a = q_ref.shape[2]
    h = _norm_mod(x_ref[...], g_ref[...].reshape(1, 1, d), sc_ref[...], sh_ref[...])
    hb = h.reshape(nb * tt, d).astype(BF16)
    p = jnp.dot(hb, w_ref[...], preferred_element_type=F32)
    q_ref[...] = (p[:, :a] * scale).reshape(nb, tt, a)
    k_ref[...] = p[:, a:2 * a].reshape(nb, tt, a)
    v_ref[...] = p[:, 2 * a:3 * a].reshape(nb, tt, a)
    u_ref[...] = p[:, 3 * a:].reshape(nb, tt, u_ref.shape[2])


def _proj_sample(x, mod, g, w, nb, scale, a):
    n, t, d = x.shape
    pw = w.shape[1] - 3 * a
    blk = lambda width: pl.BlockSpec((nb, t, width), lambda b: (b, 0, 0))
    return pl.pallas_call(
        functools.partial(_proj_sample_kernel, scale=scale),
        grid=(n // nb,),
        in_specs=[blk(d),
                  pl.BlockSpec((nb, 1, d), lambda b: (b, 0, 0)),
                  pl.BlockSpec((nb, 1, d), lambda b: (b, 0, 1)),
                  pl.BlockSpec((1, d), lambda b: (0, 0)),
                  pl.BlockSpec(w.shape, lambda b: (0, 0))],
        out_specs=[blk(a), blk(a), blk(a), blk(pw)],
        out_shape=[jax.ShapeDtypeStruct((n, t, a), F32)] * 3 + [jax.ShapeDtypeStruct((n, t, pw), F32)],
        compiler_params=pltpu.CompilerParams(dimension_semantics=("arbitrary",), vmem_limit_bytes=VMEM_LIMIT),
        name="proj_sample",
    )(x, mod, mod, g, w)


def _attn_kernel(pt_ref, bias_ref, q_ref, kt_ref, vt_ref, qs_ref, kn_ref, vn_ref, kc_ref, vc_ref, o_ref, os_ref,
                 qh_ref, carry_ref, acc_ref, kbuf, vbuf, sem, *, blk, n_pages, ch, two_pass):
    qi = pl.program_id(1)
    nq = pl.num_programs(1)
    s = pl.program_id(0) * nq + qi
    n_heads = kt_ref.shape[1]
    ring, page = kbuf.shape[0], kbuf.shape[-1]
    n_tok, width = qs_ref.shape[1], qs_ref.shape[2]
    rows = n_heads * n_tok
    nc = n_pages // ch
    total = pl.num_programs(0) * nq * nc

    def chunk_copies(g):
        seq = g // nc
        c = g % nc
        slot = lax.rem(g, ring)
        copies = []
        for i in range(ch):
            p = pt_ref[seq * n_pages + (n_pages - 1 - (c * ch + i))]
            copies.append(pltpu.make_async_copy(kc_ref.at[p], kbuf.at[slot, i], sem.at[0, slot]))
            copies.append(pltpu.make_async_copy(vc_ref.at[p], vbuf.at[slot, i], sem.at[1, slot]))
        return slot, copies

    def start_chunk(g):
        for n, cp in enumerate(chunk_copies(g)[1]):
            cp.start(priority=n % 2)

    @pl.when(s == 0)
    def _():
        for g in range(ring - 1):
            start_chunk(g)

    r_i = lax.broadcasted_iota(jnp.int32, (rows, width), 0)
    c_i = lax.broadcasted_iota(jnp.int32, (rows, width), 1)
    head_diag = (r_i // n_tok) == (c_i // HEAD_DIM)
    q_bd = jnp.where(head_diag, jnp.concatenate([qs_ref[0]] * n_heads, axis=0), 0.0).astype(BF16)
    r_p = lax.broadcasted_iota(jnp.int32, (rows, page), 0)
    l_p = lax.broadcasted_iota(jnp.int32, (rows, page), 1)
    bias_d = jnp.zeros((rows, page), F32)
    for h in range(n_heads):
        bias_d = jnp.where(r_p // n_tok == h, bias_ref[h], bias_d)
    negm_d = _neg_suffix_matrix(page, LANES)

    def new_tokens():
        pad = jnp.zeros((page - n_tok, width), F32)
        kn = jnp.concatenate([kn_ref[0], pad], axis=0).astype(BF16)
        vn = jnp.concatenate([vn_ref[0], pad], axis=0).astype(BF16)
        mask = l_p < (r_p % n_tok)
        z = lax.dot_general(q_bd, kn, _NT, preferred_element_type=F32) + bias_d
        sp = _softplus(z)
        r = _suffix_sums([jnp.where(mask, sp, 0.0)], negm_d, True)[0]
        return r[:, page:], jnp.dot(_stick_weights(z, sp, r, None, mask, page), vn, preferred_element_type=F32)

    def decode_chunk(c, state):
        g = s * nc + c
        ahead = g + (ring - 1)

        @pl.when(ahead < total)
        def _():
            start_chunk(ahead)

        slot, copies = chunk_copies(g)
        for cp in copies:
            cp.wait()
        carry, acc = state
        pages = range(ch)
        zs = [jnp.dot(q_bd, kbuf[slot, i].reshape(width, page).astype(BF16), preferred_element_type=F32) + bias_d
              for i in pages]
        sps = [_softplus(z) for z in zs]
        rs = _suffix_sums(sps, negm_d, True)
        aa = []
        for i in pages:
            aa.append(_stick_weights(zs[i], sps[i], rs[i], carry, None, page))
            carry = carry + rs[i][:, page:]
        for i in pages:
            vt = vbuf[slot, i].reshape(width, page).astype(BF16)
            acc = acc + lax.dot_general(aa[i], vt, _NT, preferred_element_type=F32)
        return carry, acc

    negm_p = _neg_suffix_matrix(blk, LANES)
    row = lax.broadcasted_iota(jnp.int32, (blk, blk), 0)
    col = lax.broadcasted_iota(jnp.int32, (blk, blk), 1)
    diag_mask = col < row
    for h in range(n_heads):
        qh_ref[h] = q_ref[0, :, h * HEAD_DIM:(h + 1) * HEAD_DIM]

    def key_block(j, mask, first):
        start = pl.multiple_of(j * blk, blk)
        heads = range(n_heads)
        zs = [jnp.dot(qh_ref[h], kt_ref[0, h, :, pl.ds(start, blk)].astype(BF16), preferred_element_type=F32)
              + bias_ref[h] for h in heads]
        sps = [_softplus(z) for z in zs]
        rs = _suffix_sums(sps if mask is None else [jnp.where(mask, sp, 0.0) for sp in sps], negm_p, two_pass)
        aa = []
        for h in heads:
            carry = None if first else carry_ref[h]
            aa.append(_stick_weights(zs[h], sps[h], rs[h], carry, mask, blk))
            carry_ref[h] = rs[h][:, blk:] if first else carry + rs[h][:, blk:]
        for h in heads:
            vt = vt_ref[0, h, :, pl.ds(start, blk)].astype(BF16)
            pv = lax.dot_general(aa[h], vt, _NT, preferred_element_type=F32)
            acc_ref[h] = pv if first else acc_ref[h] + pv

    state = decode_chunk(0, new_tokens())
    key_block(qi, diag_mask, True)

    def step(c, state):
        state = decode_chunk(c, state)

        @pl.when(c <= qi)
        def _():
            key_block(qi - c, None, False)

        return state

    _, acc_d = lax.fori_loop(1, nc, step, state)
    os_ref[0] = jnp.where(head_diag, acc_d, 0.0).reshape(n_heads, n_tok, width).sum(axis=0)
    o_ref[0] = jnp.concatenate([acc_ref[h] for h in range(n_heads)], axis=1).astype(o_ref.dtype)


def _attention(page_table, bias, q, kt4, vt4, q_s, k_s, v_s, kc, vc, blk, ch, ring):
    n, t, a = q.shape
    n_heads = kt4.shape[1]
    n_d, n_tok, _ = q_s.shape
    nq = t // blk
    n_pages = page_table.shape[1]
    assert n_d == n * nq, "one decode sequence per prompt query block"
    assert nq <= n_pages // ch, "every older key block needs a decode chunk to follow"
    page_shape = kc.shape[1:]
    tok = pl.BlockSpec((1, n_tok, a), lambda b, i, pt: (b * nq + i, 0, 0))
    qblk = pl.BlockSpec((1, blk, a), lambda b, i, pt: (b, i, 0))
    return pl.pallas_call(
        functools.partial(_attn_kernel, blk=blk, n_pages=n_pages, ch=ch, two_pass=False),
        grid_spec=pltpu.PrefetchScalarGridSpec(
            num_scalar_prefetch=1,
            grid=(n, nq),
            in_specs=[pl.BlockSpec(memory_space=pltpu.SMEM), qblk,
                      pl.BlockSpec((1,) + kt4.shape[1:], lambda b, i, pt: (b, 0, 0, 0)),
                      pl.BlockSpec((1,) + vt4.shape[1:], lambda b, i, pt: (b, 0, 0, 0)),
                      tok, tok, tok,
                      pl.BlockSpec(memory_space=pl.ANY), pl.BlockSpec(memory_space=pl.ANY)],
            out_specs=[qblk, tok],
            scratch_shapes=[pltpu.VMEM((n_heads, blk, HEAD_DIM), BF16),
                            pltpu.VMEM((n_heads, blk, LANES), F32),
                            pltpu.VMEM((n_heads, blk, HEAD_DIM), F32),
                            pltpu.VMEM((ring, ch) + page_shape, F32),
                            pltpu.VMEM((ring, ch) + page_shape, F32),
                            pltpu.SemaphoreType.DMA((2, ring))]),
        out_shape=[jax.ShapeDtypeStruct((n, t, a), BF16), jax.ShapeDtypeStruct((n_d, n_tok, a), F32)],
        compiler_params=pltpu.CompilerParams(dimension_semantics=("arbitrary", "arbitrary"),
                                             vmem_limit_bytes=VMEM_LIMIT),
        name="attention",
    )(page_table.reshape(-1), bias, q, kt4, vt4, q_s, k_s, v_s, kc, vc)


def _post_kernel(x_ref, attn_ref, u_ref, prev_ref, g1_ref, sh2_ref, sc2_ref, g2_ref, n2g_ref, fg_ref,
                 wpool_ref, pscale_ref, wo_ref, w1_ref, w2_ref, y_ref, ext_ref, *, pos0, zero_first_prev, ff_chunk,
                 final_norm):
    nb, tt, d = x_ref.shape
    m = nb * tt
    t_idx = pl.program_id(1)
    gw = wpool_ref.shape[1]

    u = u_ref[...]
    prev = prev_ref[...]
    if zero_first_prev:
        prev = jnp.where(t_idx > 0, prev, 0.0)
    ext_ref[:, 0:POOL_HALO, :] = prev
    ext_ref[:, POOL_HALO:POOL_HALO + tt, :] = u
    pos = pos0 + t_idx * tt + lax.broadcasted_iota(jnp.int32, (1, tt, gw), 1)
    parts = [attn_ref[...].reshape(m, attn_ref.shape[2]).astype(BF16)]
    for g, w in enumerate(POOL_WINDOWS):
        lo = g * gw
        ssum = ext_ref[:, POOL_HALO:POOL_HALO + tt, lo:lo + gw]
        for back in range(1, w):
            ssum = ssum + ext_ref[:, POOL_HALO - back:POOL_HALO - back + tt, lo:lo + gw]
        cnt = jnp.minimum(pos + 1, w).astype(F32)
        delta = (ssum / cnt - u[:, :, lo:lo + gw]).reshape(m, gw).astype(BF16)
        pooled = jnp.dot(delta, wpool_ref[g], preferred_element_type=F32) * pscale_ref[:, lo:lo + gw]
        parts.append(pooled.astype(BF16))
    mix = jnp.dot(jnp.concatenate(parts, axis=-1), wo_ref[...], preferred_element_type=F32)
    x1 = x_ref[...] + g1_ref[...] * mix.reshape(nb, tt, d)

    h2 = _norm_mod(x1, n2g_ref[...].reshape(1, 1, d), sc2_ref[...], sh2_ref[...]).reshape(m, d).astype(BF16)
    acc = jnp.zeros((m, d), F32)
    for c in range(w1_ref.shape[1] // ff_chunk):
        f = jnp.dot(h2, w1_ref[:, c * ff_chunk:(c + 1) * ff_chunk], preferred_element_type=F32)
        f = jnp.square(jnp.maximum(f, 0.0)).astype(BF16)
        acc = acc + jnp.dot(f, w2_ref[c * ff_chunk:(c + 1) * ff_chunk, :], preferred_element_type=F32)
    x2 = x1 + g2_ref[...] * acc.reshape(nb, tt, d)
    y_ref[...] = _rms(x2) * fg_ref[...].reshape(1, 1, d) if final_norm else x2


def _post(x, attn, u, prev, prev_map, mod, n2g, fg, wpool, pscale, wo, w1, w2, nb, tt, pos0, zero_first_prev,
          final_norm):
    n, t, d = x.shape
    a = attn.shape[2]
    pw = u.shape[2]
    tile = lambda width: pl.BlockSpec((nb, tt, width), lambda b, i: (b, i, 0))
    modspec = lambda k: pl.BlockSpec((nb, 1, d), lambda b, i: (b, 0, k))
    const2 = lambda b, i: (0, 0)
    resident = lambda arr: pl.BlockSpec(arr.shape, (lambda b, i: (0,) * arr.ndim), pipeline_mode=pl.Buffered(1))
    return pl.pallas_call(
        functools.partial(_post_kernel, pos0=pos0, zero_first_prev=zero_first_prev, ff_chunk=1024,
                          final_norm=final_norm),
        grid=(n // nb, t // tt),
        in_specs=[tile(d), tile(a), tile(pw),
                  pl.BlockSpec((nb, POOL_HALO, pw), prev_map),
                  modspec(2), modspec(3), modspec(4), modspec(5),
                  pl.BlockSpec((1, d), const2), pl.BlockSpec((1, d), const2),
                  resident(wpool), pl.BlockSpec((1, pw), const2),
                  resident(wo), resident(w1), resident(w2)],
        out_specs=tile(d),
        out_shape=jax.ShapeDtypeStruct((n, t, d), F32),
        scratch_shapes=[pltpu.VMEM((nb, POOL_HALO + tt, pw), F32)],
        compiler_params=pltpu.CompilerParams(dimension_semantics=("arbitrary", "arbitrary"),
                                             vmem_limit_bytes=VMEM_LIMIT),
        name="post",
    )(x, attn, u, prev, mod, mod, mod, mod, n2g, fg, wpool, pscale, wo, w1, w2)


def kernel(x_prompt, x_sample, c_prompt, c_sample, cache_k, cache_v, state_pool, page_table, norm1_g, norm2_g,
           w_ada, b_ada, w_in, sb_bias, w_pool, pool_scale, w_o, w_ff1, w_ff2, final_g):
    n_b, seq, d = x_prompt.shape
    n_d, dec_seq, _ = x_sample.shape
    depth = w_in.shape[0]
    n_heads = sb_bias.shape[1]
    a = n_heads * HEAD_DIM
    past_len = page_table.shape[1] * cache_k.shape[2]
    scale = HEAD_DIM ** -0.5
    fg = final_g.reshape(1, d)

    xp, xs = x_prompt, x_sample
    outs = [[] for _ in range(6)]
    for l in range(depth):
        mod = _modulation(jnp.concatenate([c_prompt, c_sample], axis=0), w_ada[l], b_ada[l].reshape(1, -1))
        mod_p = mod[:n_b].reshape(n_b, 1, -1)
        mod_s = mod[n_b:].reshape(n_d, 1, -1)
        g1 = norm1_g[l].reshape(1, d)
        g2 = norm2_g[l].reshape(1, d)
        w_in_b = w_in[l].astype(BF16)
        wq, wu = w_in_b[:, :a], w_in_b[:, 3 * a:]
        wkvt = w_in_b[:, a:3 * a].T
        wpool = w_pool[l].astype(BF16)
        pscale = pool_scale[l].reshape(1, -1)
        wo, w1, w2 = w_o[l].astype(BF16), w_ff1[l].astype(BF16), w_ff2[l].astype(BF16)
        bias = sb_bias[l]

        tt = 512
        q_p, kt_p, vt_p, u_p = _proj_prompt(xp, mod_p, g1, wq, wkvt, wu, tt, scale)
        kt4 = kt_p.reshape(n_b, n_heads, HEAD_DIM, seq)
        vt4 = vt_p.reshape(n_b, n_heads, HEAD_DIM, seq)
        q_s, k_s, v_s, u_s = _proj_sample(xs, mod_s, g1, w_in_b, 64, scale, a)
        kc = jnp.transpose(cache_k[l], (0, 2, 3, 1))
        vc = jnp.transpose(cache_v[l], (0, 2, 3, 1))
        attn_p, attn_s = _attention(page_table, bias, q_p, kt4, vt4, q_s, k_s, v_s, kc, vc, 256, 8, 5)

        halo_per_tile = tt // POOL_HALO
        xp = _post(xp, attn_p, u_p, u_p, lambda b, i: (b, jnp.maximum(i * halo_per_tile - 1, 0), 0),
                   mod_p, g2, fg, wpool, pscale, wo, w1, w2, 1, tt, 0, True, l == depth - 1)
        outs[0].append(jnp.transpose(kt4, (0, 3, 1, 2)))
        outs[1].append(jnp.transpose(vt4, (0, 3, 1, 2)))
        outs[2].append(u_p[:, seq - (POOL_HALO - 1):])

        prev_s = jnp.concatenate([jnp.zeros((n_d, 1, u_s.shape[2]), F32), state_pool[l]], axis=1)
        xs = _post(xs, attn_s, u_s, prev_s, lambda b, i: (b, 0, 0),
                   mod_s, g2, fg, wpool, pscale, wo, w1, w2, 64, dec_seq, past_len, False, l == depth - 1)
        outs[3].append(k_s.reshape(n_d, dec_seq, n_heads, HEAD_DIM))
        outs[4].append(v_s.reshape(n_d, dec_seq, n_heads, HEAD_DIM))
        outs[5].append(jnp.concatenate([state_pool[l], u_s], axis=1)[:, -(POOL_HALO - 1):])

    stk = [jnp.stack(o) for o in outs]
    return (xp, xs, stk[0], stk[1], stk[2], stk[3], stk[4], stk[5])
```

```python
import functools

import jax
import jax.numpy as jnp
from jax import lax
from jax.experimental import pallas as pl
from jax.experimental.pallas import tpu as pltpu

F32 = jnp.float32
BF16 = jnp.bfloat16

HEAD_DIM = 64
POOL_WINDOWS = (2, 4, 8, 16)
POOL_HALO = 16
RMS_EPS = 1e-6
LANES = 128
VMEM_LIMIT = 56 * 1024 * 1024

_NT = (((1,), (1,)), ((), ()))


def _softplus(z):
    return jnp.maximum(z, 0.0) + jnp.log(1.0 + jnp.exp(-jnp.abs(z)))


def _rms(x):
    return x * lax.rsqrt(jnp.mean(x * x, axis=-1, keepdims=True) + RMS_EPS)


def _neg_suffix_matrix(n, extra):
    s = lax.broadcasted_iota(jnp.int32, (n, n + extra), 0)
    j = lax.broadcasted_iota(jnp.int32, (n, n + extra), 1)
    return jnp.where((s > j) | (j >= n), -1.0, 0.0).astype(BF16)


def _suffix_sums(sps, negm, two_pass):
    his = [sp.astype(BF16) for sp in sps]
    rs = [jnp.dot(hi, negm, preferred_element_type=F32) for hi in his]
    if two_pass:
        los = [(sp - hi.astype(F32)).astype(BF16) for sp, hi in zip(sps, his)]
        rs = [r + jnp.dot(lo, negm, preferred_element_type=F32) for r, lo in zip(rs, los)]
    return rs


def _stick_weights(z, sp, r, carry, mask, nk):
    later = r[:, :nk]
    if carry is not None:
        later = later + jnp.concatenate([carry] * (nk // LANES), axis=1)
    a = jnp.exp(z - sp + later)
    if mask is not None:
        a = jnp.where(mask, a, 0.0)
    return a.astype(BF16)


def _mod_kernel(c_ref, w_ref, b_ref, o_ref):
    c = c_ref[...]
    s = (c * (1.0 / (1.0 + jnp.exp(-c)))).astype(BF16)
    o_ref[...] = jnp.dot(s, w_ref[...].astype(BF16), preferred_element_type=F32) + b_ref[...]


def _modulation(c_all, w_ada, b_ada):
    n, d = c_all.shape
    n_out = w_ada.shape[1]
    return pl.pallas_call(
        _mod_kernel,
        grid=(n_out // d,),
        in_specs=[pl.BlockSpec((n, d), lambda j: (0, 0)),
                  pl.BlockSpec((d, d), lambda j: (0, j)),
                  pl.BlockSpec((1, d), lambda j: (0, j))],
        out_specs=pl.BlockSpec((n, d), lambda j: (0, j)),
        out_shape=jax.ShapeDtypeStruct((n, n_out), F32),
        name="mod",
    )(c_all, w_ada, b_ada)


def _norm_mod(x, g, sc, sh):
    return (_rms(x) * g) * (1.0 + sc) + sh


def _proj_prompt_kernel(x_ref, sh_ref, sc_ref, g_ref, wq_ref, wkvt_ref, wu_ref, q_ref, kt_ref, vt_ref, u_ref,
                        *, scale):
    hb = _norm_mod(x_ref[0], g_ref[...], sc_ref[0], sh_ref[0]).astype(BF16)
    q = jnp.dot(hb, wq_ref[...], preferred_element_type=F32)
    q_ref[0] = (q * scale).astype(BF16)
    u_ref[0] = jnp.dot(hb, wu_ref[...], preferred_element_type=F32)
    kvt = lax.dot_general(wkvt_ref[...], hb, _NT, preferred_element_type=F32)
    a = kt_ref.shape[1]
    kt_ref[0] = kvt[:a]
    vt_ref[0] = kvt[a:]


def _proj_prompt(x, mod, g, wq, wkvt, wu, tt, scale):
    n, t, d = x.shape
    a = wq.shape[1]
    const = lambda b, i: (0, 0)
    return pl.pallas_call(
        functools.partial(_proj_prompt_kernel, scale=scale),
        grid=(n, t // tt),
        in_specs=[pl.BlockSpec((1, tt, d), lambda b, i: (b, i, 0)),
                  pl.BlockSpec((1, 1, d), lambda b, i: (b, 0, 0)),
                  pl.BlockSpec((1, 1, d), lambda b, i: (b, 0, 1)),
                  pl.BlockSpec((1, d), const),
                  pl.BlockSpec(wq.shape, const),
                  pl.BlockSpec(wkvt.shape, const),
                  pl.BlockSpec(wu.shape, const)],
        out_specs=[pl.BlockSpec((1, tt, a), lambda b, i: (b, i, 0)),
                   pl.BlockSpec((1, a, tt), lambda b, i: (b, 0, i)),
                   pl.BlockSpec((1, a, tt), lambda b, i: (b, 0, i)),
                   pl.BlockSpec((1, tt, wu.shape[1]), lambda b, i: (b, i, 0))],
        out_shape=[jax.ShapeDtypeStruct((n, t, a), BF16),
                   jax.ShapeDtypeStruct((n, a, t), F32),
                   jax.ShapeDtypeStruct((n, a, t), F32),
                   jax.ShapeDtypeStruct((n, t, wu.shape[1]), F32)],
        compiler_params=pltpu.CompilerParams(dimension_semantics=("arbitrary", "arbitrary"),
                                             vmem_limit_bytes=VMEM_LIMIT),
        name="proj_prompt",
    )(x, mod, mod, g, wq, wkvt, wu)


def _proj_sample_kernel(x_ref, sh_ref, sc_ref, g_ref, w_ref, q_ref, k_ref, v_ref, u_ref, *, scale):
    nb, tt, d = x_ref.shape
    a = q_ref.shape[2]
    h = _norm_mod(x_ref[...], g_ref[...].reshape(1, 1, d), sc_ref[...], sh_ref[...])
    hb = h.reshape(nb * tt, d).astype(BF16)
    p = jnp.dot(hb, w_ref[...], preferred_element_type=F32)
    q_ref[...] = (p[:, :a] * scale).reshape(nb, tt, a)
    k_ref[...] = p[:, a:2 * a].reshape(nb, tt, a)
    v_ref[...] = p[:, 2 * a:3 * a].reshape(nb, tt, a)
    u_ref[...] = p[:, 3 * a:].reshape(nb, tt, u_ref.shape[2])


def _proj_sample(x, mod, g, w, nb, scale, a):
    n, t, d = x.shape
    pw = w.shape[1] - 3 * a
    blk = lambda width: pl.BlockSpec((nb, t, width), lambda b: (b, 0, 0))
    return pl.pallas_call(
        functools.partial(_proj_sample_kernel, scale=scale),
        grid=(n // nb,),
        in_specs=[blk(d),
                  pl.BlockSpec((nb, 1, d), lambda b: (b, 0, 0)),
                  pl.BlockSpec((nb, 1, d), lambda b: (b, 0, 1)),
                  pl.BlockSpec((1, d), lambda b: (0, 0)),
                  pl.BlockSpec(w.shape, lambda b: (0, 0))],
        out_specs=[blk(a), blk(a), blk(a), blk(pw)],
        out_shape=[jax.ShapeDtypeStruct((n, t, a), F32)] * 3 + [jax.ShapeDtypeStruct((n, t, pw), F32)],
        compiler_params=pltpu.CompilerParams(dimension_semantics=("arbitrary",), vmem_limit_bytes=VMEM_LIMIT),
        name="proj_sample",
    )(x, mod, mod, g, w)


def _query_block_order(i, nq):
    return jnp.where(i % 2 == 0, nq - 1 - i // 2, i // 2)


def _attn_kernel(pt_ref, bias_ref, q_ref, kt_ref, vt_ref, qs_ref, kn_ref, vn_ref, kc_ref, vc_ref, o_ref, os_ref,
                 qh_ref, carry_ref, acc_ref, dcarry_ref, dacc_ref, kbuf, vbuf, sem, *, blk, n_pages, ch, two_pass):
    nq = pl.num_programs(1)
    qi = _query_block_order(pl.program_id(1), nq)
    s = pl.program_id(0) * nq + pl.program_id(1)
    n_heads = kt_ref.shape[1]
    ring, page = kbuf.shape[0], kbuf.shape[-1]
    n_tok, width = qs_ref.shape[1], qs_ref.shape[2]
    rows = n_heads * n_tok
    nc = n_pages // ch
    total = pl.num_programs(0) * nq * nc

    def chunk_copies(g):
        seq = g // nc
        c = g % nc
        slot = lax.rem(g, ring)
        copies = []
        for i in range(ch):
            p = pt_ref[seq * n_pages + (n_pages - 1 - (c * ch + i))]
            copies.append(pltpu.make_async_copy(kc_ref.at[p], kbuf.at[slot, i], sem.at[0, slot]))
            copies.append(pltpu.make_async_copy(vc_ref.at[p], vbuf.at[slot, i], sem.at[1, slot]))
        return slot, copies

    def start_chunk(g):
        for n, cp in enumerate(chunk_copies(g)[1]):
            cp.start(priority=n % 2)

    @pl.when(s == 0)
    def _():
        for g in range(ring - 1):
            start_chunk(g)

    r_i = lax.broadcasted_iota(jnp.int32, (rows, width), 0)
    c_i = lax.broadcasted_iota(jnp.int32, (rows, width), 1)
    head_diag = (r_i // n_tok) == (c_i // HEAD_DIM)
    q_bd = jnp.where(head_diag, jnp.concatenate([qs_ref[0]] * n_heads, axis=0), 0.0).astype(BF16)
    r_p = lax.broadcasted_iota(jnp.int32, (rows, page), 0)
    l_p = lax.broadcasted_iota(jnp.int32, (rows, page), 1)
    bias_d = jnp.zeros((rows, page), F32)
    for h in range(n_heads):
        bias_d = jnp.where(r_p // n_tok == h, bias_ref[h], bias_d)
    negm_d = _neg_suffix_matrix(page, LANES)

    def new_tokens():
        pad = jnp.zeros((page - n_tok, width), F32)
        kn = jnp.concatenate([kn_ref[0], pad], axis=0).astype(BF16)
        vn = jnp.concatenate([vn_ref[0], pad], axis=0).astype(BF16)
        mask = l_p < (r_p % n_tok)
        z = lax.dot_general(q_bd, kn, _NT, preferred_element_type=F32) + bias_d
        sp = _softplus(z)
        r = _suffix_sums([jnp.where(mask, sp, 0.0)], negm_d, True)[0]
        return r[:, page:], jnp.dot(_stick_weights(z, sp, r, None, mask, page), vn, preferred_element_type=F32)

    def fetch_chunk(c):
        g = s * nc + c
        ahead = g + (ring - 1)

        @pl.when(ahead < total)
        def _():
            start_chunk(ahead)

        slot, copies = chunk_copies(g)
        for cp in copies:
            cp.wait()
        return slot

    negm_p = _neg_suffix_matrix(blk, LANES)
    row = lax.broadcasted_iota(jnp.int32, (blk, blk), 0)
    col = lax.broadcasted_iota(jnp.int32, (blk, blk), 1)
    diag_mask = col < row
    for h in range(n_heads):
        qh_ref[h] = q_ref[0, :, h * HEAD_DIM:(h + 1) * HEAD_DIM]

    def attend(slot, j, mask=None, first=False):
        pages = range(ch)
        heads = range(n_heads) if j is not None else ()
        start = pl.multiple_of(j * blk, blk) if j is not None else None
        zd = [jnp.dot(q_bd, kbuf[slot, i].reshape(width, page).astype(BF16), preferred_element_type=F32) + bias_d
              for i in pages]
        zp = [jnp.dot(qh_ref[h], kt_ref[0, h, :, pl.ds(start, blk)].astype(BF16), preferred_element_type=F32)
              + bias_ref[h] for h in heads]
        spd = [_softplus(z) for z in zd]
        spp = [_softplus(z) for z in zp]
        rd = _suffix_sums(spd, negm_d, True)
        rp = _suffix_sums(spp if mask is None else [jnp.where(mask, sp, 0.0) for sp in spp], negm_p, two_pass)
        carry = dcarry_ref[...]
        ad = []
        for i in pages:
            ad.append(_stick_weights(zd[i], spd[i], rd[i], carry, None, page))
            carry = carry + rd[i][:, page:]
        dcarry_ref[...] = carry
        ap = []
        for h in heads:
            carry = None if first else carry_ref[h]
            ap.append(_stick_weights(zp[h], spp[h], rp[h], carry, mask, blk))
            carry_ref[h] = rp[h][:, blk:] if first else carry + rp[h][:, blk:]
        acc = dacc_ref[...]
        for i in pages:
            vt = vbuf[slot, i].reshape(width, page).astype(BF16)
            acc = acc + lax.dot_general(ad[i], vt, _NT, preferred_element_type=F32)
        dacc_ref[...] = acc
        for h in heads:
            vt = vt_ref[0, h, :, pl.ds(start, blk)].astype(BF16)
            pv = lax.dot_general(ap[h], vt, _NT, preferred_element_type=F32)
            acc_ref[h] = pv if first else acc_ref[h] + pv

    dcarry_ref[...], dacc_ref[...] = new_tokens()
    attend(fetch_chunk(0), qi, diag_mask, True)

    def step(c, _):
        slot = fetch_chunk(c)

        @pl.when(c <= qi)
        def _():
            attend(slot, qi - c)

        @pl.when(c > qi)
        def _():
            attend(slot, None)

        return 0

    lax.fori_loop(1, nc, step, 0)
    os_ref[0] = jnp.where(head_diag, dacc_ref[...], 0.0).reshape(n_heads, n_tok, width).sum(axis=0)
    o_ref[0] = jnp.concatenate([acc_ref[h] for h in range(n_heads)], axis=1).astype(o_ref.dtype)


def _attention(page_table, bias, q, kt4, vt4, q_s, k_s, v_s, kc, vc, blk, ch, ring):
    n, t, a = q.shape
    n_heads = kt4.shape[1]
    n_d, n_tok, _ = q_s.shape
    nq = t // blk
    n_pages = page_table.shape[1]
    assert n_d == n * nq, "one decode sequence per prompt query block"
    assert nq <= n_pages // ch, "every older key block needs a decode chunk to follow"
    page_shape = kc.shape[1:]
    tok = pl.BlockSpec((1, n_tok, a), lambda b, i, pt: (b * nq + i, 0, 0))
    qblk = pl.BlockSpec((1, blk, a), lambda b, i, pt: (b, _query_block_order(i, nq), 0))
    return pl.pallas_call(
        functools.partial(_attn_kernel, blk=blk, n_pages=n_pages, ch=ch, two_pass=False),
        grid_spec=pltpu.PrefetchScalarGridSpec(
            num_scalar_prefetch=1,
            grid=(n, nq),
            in_specs=[pl.BlockSpec(memory_space=pltpu.SMEM), qblk,
                      pl.BlockSpec((1,) + kt4.shape[1:], lambda b, i, pt: (b, 0, 0, 0)),
                      pl.BlockSpec((1,) + vt4.shape[1:], lambda b, i, pt: (b, 0, 0, 0)),
                      tok, tok, tok,
                      pl.BlockSpec(memory_space=pl.ANY), pl.BlockSpec(memory_space=pl.ANY)],
            out_specs=[qblk, tok],
            scratch_shapes=[pltpu.VMEM((n_heads, blk, HEAD_DIM), BF16),
                            pltpu.VMEM((n_heads, blk, LANES), F32),
                            pltpu.VMEM((n_heads, blk, HEAD_DIM), F32),
                            pltpu.VMEM((n_heads * n_tok, LANES), F32),
                            pltpu.VMEM((n_heads * n_tok, a), F32),
                            pltpu.VMEM((ring, ch) + page_shape, F32),
                            pltpu.VMEM((ring, ch) + page_shape, F32),
                            pltpu.SemaphoreType.DMA((2, ring))]),
        out_shape=[jax.ShapeDtypeStruct((n, t, a), BF16), jax.ShapeDtypeStruct((n_d, n_tok, a), F32)],
        compiler_params=pltpu.CompilerParams(dimension_semantics=("arbitrary", "arbitrary"),
                                             vmem_limit_bytes=VMEM_LIMIT),
        name="attention",
    )(page_table.reshape(-1), bias, q, kt4, vt4, q_s, k_s, v_s, kc, vc)


def _post_kernel(x_ref, attn_ref, u_ref, prev_ref, g1_ref, sh2_ref, sc2_ref, g2_ref, n2g_ref, fg_ref,
                 wpool_ref, pscale_ref, wo_ref, w1_ref, w2_ref, y_ref, ext_ref, *, pos0, zero_first_prev, ff_chunk,
                 final_norm):
    nb, tt, d = x_ref.shape
    m = nb * tt
    t_idx = pl.program_id(1)
    gw = wpool_ref.shape[1]

    u = u_ref[...]
    prev = prev_ref[...]
    if zero_first_prev:
        prev = jnp.where(t_idx > 0, prev, 0.0)
    ext_ref[:, 0:POOL_HALO, :] = prev
    ext_ref[:, POOL_HALO:POOL_HALO + tt, :] = u
    pos = pos0 + t_idx * tt + lax.broadcasted_iota(jnp.int32, (1, tt, gw), 1)
    parts = [attn_ref[...].reshape(m, attn_ref.shape[2]).astype(BF16)]
    for g, w in enumerate(POOL_WINDOWS):
        lo = g * gw
        ssum = ext_ref[:, POOL_HALO:POOL_HALO + tt, lo:lo + gw]
        for back in range(1, w):
            ssum = ssum + ext_ref[:, POOL_HALO - back:POOL_HALO - back + tt, lo:lo + gw]
        cnt = jnp.minimum(pos + 1, w).astype(F32)
        delta = (ssum / cnt - u[:, :, lo:lo + gw]).reshape(m, gw).astype(BF16)
        pooled = jnp.dot(delta, wpool_ref[g], preferred_element_type=F32) * pscale_ref[:, lo:lo + gw]
        parts.append(pooled.astype(BF16))
    mix = jnp.dot(jnp.concatenate(parts, axis=-1), wo_ref[...], preferred_element_type=F32)
    x1 = x_ref[...] + g1_ref[...] * mix.reshape(nb, tt, d)

    h2 = _norm_mod(x1, n2g_ref[...].reshape(1, 1, d), sc2_ref[...], sh2_ref[...]).reshape(m, d).astype(BF16)
    acc = jnp.zeros((m, d), F32)
    for c in range(w1_ref.shape[1] // ff_chunk):
        f = jnp.dot(h2, w1_ref[:, c * ff_chunk:(c + 1) * ff_chunk], preferred_element_type=F32)
        f = jnp.square(jnp.maximum(f, 0.0)).astype(BF16)
        acc = acc + jnp.dot(f, w2_ref[c * ff_chunk:(c + 1) * ff_chunk, :], preferred_element_type=F32)
    x2 = x1 + g2_ref[...] * acc.reshape(nb, tt, d)
    y_ref[...] = _rms(x2) * fg_ref[...].reshape(1, 1, d) if final_norm else x2


def _post(x, attn, u, prev, prev_map, mod, n2g, fg, wpool, pscale, wo, w1, w2, nb, tt, pos0, zero_first_prev,
          final_norm):
    n, t, d = x.shape
    a = attn.shape[2]
    pw = u.shape[2]
    tile = lambda width: pl.BlockSpec((nb, tt, width), lambda b, i: (b, i, 0))
    modspec = lambda k: pl.BlockSpec((nb, 1, d), lambda b, i: (b, 0, k))
    const2 = lambda b, i: (0, 0)
    resident = lambda arr: pl.BlockSpec(arr.shape, (lambda b, i: (0,) * arr.ndim), pipeline_mode=pl.Buffered(1))
    return pl.pallas_call(
        functools.partial(_post_kernel, pos0=pos0, zero_first_prev=zero_first_prev, ff_chunk=1024,
                          final_norm=final_norm),
        grid=(n // nb, t // tt),
        in_specs=[tile(d), tile(a), tile(pw),
                  pl.BlockSpec((nb, POOL_HALO, pw), prev_map),
                  modspec(2), modspec(3), modspec(4), modspec(5),
                  pl.BlockSpec((1, d), const2), pl.BlockSpec((1, d), const2),
                  resident(wpool), pl.BlockSpec((1, pw), const2),
                  resident(wo), resident(w1), resident(w2)],
        out_specs=tile(d),
        out_shape=jax.ShapeDtypeStruct((n, t, d), F32),
        scratch_shapes=[pltpu.VMEM((nb, POOL_HALO + tt, pw), F32)],
        compiler_params=pltpu.CompilerParams(dimension_semantics=("arbitrary", "arbitrary"),
                                             vmem_limit_bytes=VMEM_LIMIT),
        name="post",
    )(x, attn, u, prev, mod, mod, mod, mod, n2g, fg, wpool, pscale, wo, w1, w2)


def kernel(x_prompt, x_sample, c_prompt, c_sample, cache_k, cache_v, state_pool, page_table, norm1_g, norm2_g,
           w_ada, b_ada, w_in, sb_bias, w_pool, pool_scale, w_o, w_ff1, w_ff2, final_g):
    n_b, seq, d = x_prompt.shape
    n_d, dec_seq, _ = x_sample.shape
    depth = w_in.shape[0]
    n_heads = sb_bias.shape[1]
    a = n_heads * HEAD_DIM
    past_len = page_table.shape[1] * cache_k.shape[2]
    scale = HEAD_DIM ** -0.5
    fg = final_g.reshape(1, d)

    xp, xs = x_prompt, x_sample
    outs = [[] for _ in range(6)]
    for l in range(depth):
        mod = _modulation(jnp.concatenate([c_prompt, c_sample], axis=0), w_ada[l], b_ada[l].reshape(1, -1))
        mod_p = mod[:n_b].reshape(n_b, 1, -1)
        mod_s = mod[n_b:].reshape(n_d, 1, -1)
        g1 = norm1_g[l].reshape(1, d)
        g2 = norm2_g[l].reshape(1, d)
        w_in_b = w_in[l].astype(BF16)
        wq, wu = w_in_b[:, :a], w_in_b[:, 3 * a:]
        wkvt = w_in_b[:, a:3 * a].T
        wpool = w_pool[l].astype(BF16)
        pscale = pool_scale[l].reshape(1, -1)
        wo, w1, w2 = w_o[l].astype(BF16), w_ff1[l].astype(BF16), w_ff2[l].astype(BF16)
        bias = sb_bias[l]

        tt = 512
        q_p, kt_p, vt_p, u_p = _proj_prompt(xp, mod_p, g1, wq, wkvt, wu, tt, scale)
        kt4 = kt_p.reshape(n_b, n_heads, HEAD_DIM, seq)
        vt4 = vt_p.reshape(n_b, n_heads, HEAD_DIM, seq)
        q_s, k_s, v_s, u_s = _proj_sample(xs, mod_s, g1, w_in_b, 64, scale, a)
        kc = jnp.transpose(cache_k[l], (0, 2, 3, 1))
        vc = jnp.transpose(cache_v[l], (0, 2, 3, 1))
        attn_p, attn_s = _attention(page_table, bias, q_p, kt4, vt4, q_s, k_s, v_s, kc, vc, 256, 8, 5)

        halo_per_tile = tt // POOL_HALO
        xp = _post(xp, attn_p, u_p, u_p, lambda b, i: (b, jnp.maximum(i * halo_per_tile - 1, 0), 0),
                   mod_p, g2, fg, wpool, pscale, wo, w1, w2, 1, tt, 0, True, l == depth - 1)
        outs[0].append(jnp.transpose(kt4, (0, 3, 1, 2)))
        outs[1].append(jnp.transpose(vt4, (0, 3, 1, 2)))
        outs[2].append(u_p[:, seq - (POOL_HALO - 1):])

        prev_s = jnp.concatenate([jnp.zeros((n_d, 1, u_s.shape[2]), F32), state_pool[l]], axis=1)
        xs = _post(xs, attn_s, u_s, prev_s, lambda b, i: (b, 0, 0),
                   mod_s, g2, fg, wpool, pscale, wo, w1, w2, 64, dec_seq, past_len, False, l == depth - 1)
        outs[3].append(k_s.reshape(n_d, dec_seq, n_heads, HEAD_DIM))
        outs[4].append(v_s.reshape(n_d, dec_seq, n_heads, HEAD_DIM))
        outs[5].append(jnp.concatenate([state_pool[l], u_s], axis=1)[:, -(POOL_HALO - 1):])

    stk = [jnp.stack(o) for o in outs]
    return (xp, xs, stk[0], stk[1], stk[2], stk[3], stk[4], stk[5])
```

```python
import functools

import jax
import jax.numpy as jnp
from jax import lax
from jax.experimental import pallas as pl
from jax.experimental.pallas import tpu as pltpu

F32 = jnp.float32
BF16 = jnp.bfloat16

HEAD_DIM = 64
POOL_WINDOWS = (2, 4, 8, 16)
POOL_HALO = 16
RMS_EPS = 1e-6
LANES = 128
VMEM_LIMIT = 56 * 1024 * 1024

_NT = (((1,), (1,)), ((), ()))


def _softplus(z):
    return jnp.maximum(z, 0.0) + jnp.log(1.0 + jnp.exp(-jnp.abs(z)))


def _rms(x):
    return x * lax.rsqrt(jnp.mean(x * x, axis=-1, keepdims=True) + RMS_EPS)


def _neg_suffix_matrix(n, extra):
    s = lax.broadcasted_iota(jnp.int32, (n, n + extra), 0)
    j = lax.broadcasted_iota(jnp.int32, (n, n + extra), 1)
    return jnp.where((s > j) | (j >= n), -1.0, 0.0).astype(BF16)


def _suffix_sums(sps, negm, two_pass):
    his = [sp.astype(BF16) for sp in sps]
    rs = [jnp.dot(hi, negm, preferred_element_type=F32) for hi in his]
    if two_pass:
        los = [(sp - hi.astype(F32)).astype(BF16) for sp, hi in zip(sps, his)]
        rs = [r + jnp.dot(lo, negm, preferred_element_type=F32) for r, lo in zip(rs, los)]
    return rs


def _stick_weights(log_beta, r, carry, mask, nk):
    later = r[:, :nk]
    if carry is not None:
        later = later + jnp.concatenate([carry] * (nk // LANES), axis=1)
    a = jnp.exp(log_beta + later)
    if mask is not None:
        a = jnp.where(mask, a, 0.0)
    return a.astype(BF16)


def _mod_kernel(c_ref, w_ref, b_ref, o_ref):
    c = c_ref[...]
    s = (c * (1.0 / (1.0 + jnp.exp(-c)))).astype(BF16)
    o_ref[...] = jnp.dot(s, w_ref[...].astype(BF16), preferred_element_type=F32) + b_ref[...]


def _modulation(c_all, w_ada, b_ada):
    n, d = c_all.shape
    n_out = w_ada.shape[1]
    return pl.pallas_call(
        _mod_kernel,
        grid=(n_out // d,),
        in_specs=[pl.BlockSpec((n, d), lambda j: (0, 0)),
                  pl.BlockSpec((d, d), lambda j: (0, j)),
                  pl.BlockSpec((1, d), lambda j: (0, j))],
        out_specs=pl.BlockSpec((n, d), lambda j: (0, j)),
        out_shape=jax.ShapeDtypeStruct((n, n_out), F32),
        name="mod",
    )(c_all, w_ada, b_ada)


def _norm_mod(x, g, sc, sh):
    return (_rms(x) * g) * (1.0 + sc) + sh


def _proj_prompt_kernel(x_ref, sh_ref, sc_ref, g_ref, wq_ref, wkvt_ref, wu_ref, q_ref, kt_ref, vt_ref, u_ref,
                        *, scale):
    hb = _norm_mod(x_ref[0], g_ref[...], sc_ref[0], sh_ref[0]).astype(BF16)
    q = jnp.dot(hb, wq_ref[...], preferred_element_type=F32)
    q_ref[0] = (q * scale).astype(BF16)
    u_ref[0] = jnp.dot(hb, wu_ref[...], preferred_element_type=F32)
    kvt = lax.dot_general(wkvt_ref[...], hb, _NT, preferred_element_type=F32)
    a = kt_ref.shape[1]
    kt_ref[0] = kvt[:a]
    vt_ref[0] = kvt[a:]


def _proj_prompt(x, mod, g, wq, wkvt, wu, tt, scale):
    n, t, d = x.shape
    a = wq.shape[1]
    const = lambda b, i: (0, 0)
    return pl.pallas_call(
        functools.partial(_proj_prompt_kernel, scale=scale),
        grid=(n, t // tt),
        in_specs=[pl.BlockSpec((1, tt, d), lambda b, i: (b, i, 0)),
                  pl.BlockSpec((1, 1, d), lambda b, i: (b, 0, 0)),
                  pl.BlockSpec((1, 1, d), lambda b, i: (b, 0, 1)),
                  pl.BlockSpec((1, d), const),
                  pl.BlockSpec(wq.shape, const),
                  pl.BlockSpec(wkvt.shape, const),
                  pl.BlockSpec(wu.shape, const)],
        out_specs=[pl.BlockSpec((1, tt, a), lambda b, i: (b, i, 0)),
                   pl.BlockSpec((1, a, tt), lambda b, i: (b, 0, i)),
                   pl.BlockSpec((1, a, tt), lambda b, i: (b, 0, i)),
                   pl.BlockSpec((1, tt, wu.shape[1]), lambda b, i: (b, i, 0))],
        out_shape=[jax.ShapeDtypeStruct((n, t, a), BF16),
                   jax.ShapeDtypeStruct((n, a, t), F32),
                   jax.ShapeDtypeStruct((n, a, t), F32),
                   jax.ShapeDtypeStruct((n, t, wu.shape[1]), F32)],
        compiler_params=pltpu.CompilerParams(dimension_semantics=("arbitrary", "arbitrary"),
                                             vmem_limit_bytes=VMEM_LIMIT),
        name="proj_prompt",
    )(x, mod, mod, g, wq, wkvt, wu)


def _proj_sample_kernel(x_ref, sh_ref, sc_ref, g_ref, w_ref, q_ref, k_ref, v_ref, u_ref, *, scale):
    nb, tt, d = x_ref.shape
    a = q_ref.shape[2]
    h = _norm_mod(x_ref[...], g_ref[...].reshape(1, 1, d), sc_ref[...], sh_ref[...])
    hb = h.reshape(nb * tt, d).astype(BF16)
    p = jnp.dot(hb, w_ref[...], preferred_element_type=F32)
    q_ref[...] = (p[:, :a] * scale).reshape(nb, tt, a)
    k_ref[...] = p[:, a:2 * a].reshape(nb, tt, a)
    v_ref[...] = p[:, 2 * a:3 * a].reshape(nb, tt, a)
    u_ref[...] = p[:, 3 * a:].reshape(nb, tt, u_ref.shape[2])


def _proj_sample(x, mod, g, w, nb, scale, a):
    n, t, d = x.shape
    pw = w.shape[1] - 3 * a
    blk = lambda width: pl.BlockSpec((nb, t, width), lambda b: (b, 0, 0))
    return pl.pallas_call(
        functools.partial(_proj_sample_kernel, scale=scale),
        grid=(n // nb,),
        in_specs=[blk(d),
                  pl.BlockSpec((nb, 1, d), lambda b: (b, 0, 0)),
                  pl.BlockSpec((nb, 1, d), lambda b: (b, 0, 1)),
                  pl.BlockSpec((1, d), lambda b: (0, 0)),
                  pl.BlockSpec(w.shape, lambda b: (0, 0))],
        out_specs=[blk(a), blk(a), blk(a), blk(pw)],
        out_shape=[jax.ShapeDtypeStruct((n, t, a), F32)] * 3 + [jax.ShapeDtypeStruct((n, t, pw), F32)],
        compiler_params=pltpu.CompilerParams(dimension_semantics=("arbitrary",), vmem_limit_bytes=VMEM_LIMIT),
        name="proj_sample",
    )(x, mod, mod, g, w)


def _query_block_order(i, nq):
    return jnp.where(i % 2 == 0, nq - 1 - i // 2, i // 2)


def _attn_kernel(pt_ref, bias_ref, q_ref, kt_ref, vt_ref, qs_ref, kn_ref, vn_ref, kc_ref, vc_ref, o_ref, os_ref,
                 qh_ref, carry_ref, acc_ref, dcarry_ref, dacc_ref, kbuf, vbuf, sem, *, blk, n_pages, ch, two_pass):
    nq = pl.num_programs(1)
    qi = _query_block_order(pl.program_id(1), nq)
    s = pl.program_id(0) * nq + pl.program_id(1)
    n_heads = kt_ref.shape[1]
    ring, page = kbuf.shape[0], kbuf.shape[-1]
    n_tok, width = qs_ref.shape[1], qs_ref.shape[2]
    rows = n_heads * n_tok
    nc = n_pages // ch
    total = pl.num_programs(0) * nq * nc

    def chunk_copies(g):
        seq = g // nc
        c = g % nc
        slot = lax.rem(g, ring)
        copies = []
        for i in range(ch):
            p = pt_ref[seq * n_pages + (n_pages - 1 - (c * ch + i))]
            copies.append(pltpu.make_async_copy(kc_ref.at[p], kbuf.at[slot, i], sem.at[0, slot]))
            copies.append(pltpu.make_async_copy(vc_ref.at[p], vbuf.at[slot, i], sem.at[1, slot]))
        return slot, copies

    def start_chunk(g):
        for n, cp in enumerate(chunk_copies(g)[1]):
            cp.start(priority=n % 2)

    @pl.when(s == 0)
    def _():
        for g in range(ring - 1):
            start_chunk(g)

    r_i = lax.broadcasted_iota(jnp.int32, (rows, width), 0)
    c_i = lax.broadcasted_iota(jnp.int32, (rows, width), 1)
    head_diag = (r_i // n_tok) == (c_i // HEAD_DIM)
    q_bd = jnp.where(head_diag, jnp.concatenate([qs_ref[0]] * n_heads, axis=0), 0.0).astype(BF16)
    r_p = lax.broadcasted_iota(jnp.int32, (rows, page), 0)
    l_p = lax.broadcasted_iota(jnp.int32, (rows, page), 1)
    bias_d = jnp.zeros((rows, page), F32)
    for h in range(n_heads):
        bias_d = jnp.where(r_p // n_tok == h, bias_ref[h], bias_d)
    negm_d = _neg_suffix_matrix(page, LANES)

    def new_tokens():
        pad = jnp.zeros((page - n_tok, width), F32)
        kn = jnp.concatenate([kn_ref[0], pad], axis=0).astype(BF16)
        vn = jnp.concatenate([vn_ref[0], pad], axis=0).astype(BF16)
        mask = l_p < (r_p % n_tok)
        z = lax.dot_general(q_bd, kn, _NT, preferred_element_type=F32) + bias_d
        sp = _softplus(z)
        r = _suffix_sums([jnp.where(mask, sp, 0.0)], negm_d, True)[0]
        return r[:, page:], jnp.dot(_stick_weights(z - sp, r, None, mask, page), vn, preferred_element_type=F32)

    def fetch_chunk(c):
        g = s * nc + c
        ahead = g + (ring - 1)

        @pl.when(ahead < total)
        def _():
            start_chunk(ahead)

        slot, copies = chunk_copies(g)
        for cp in copies:
            cp.wait()
        return slot

    negm_p = _neg_suffix_matrix(blk, LANES)
    row = lax.broadcasted_iota(jnp.int32, (blk, blk), 0)
    col = lax.broadcasted_iota(jnp.int32, (blk, blk), 1)
    diag_mask = col < row
    lane = lax.broadcasted_iota(jnp.int32, (blk, LANES), 1)
    for h in range(n_heads):
        pair = q_ref[0, :, (h // 2) * LANES:(h // 2 + 1) * LANES]
        q_h = pair if h % 2 == 0 else jnp.concatenate([pair[:, HEAD_DIM:], pair[:, :HEAD_DIM]], axis=1)
        b = jnp.full((blk, LANES), bias_ref[h], F32)
        b1 = b.astype(BF16).astype(F32)
        b2 = (b - b1).astype(BF16).astype(F32)
        b3 = (b - b1 - b2).astype(BF16).astype(F32)
        cols = jnp.where(lane == HEAD_DIM, b1, jnp.where(lane == HEAD_DIM + 1, b2,
                                                          jnp.where(lane == HEAD_DIM + 2, b3, 0.0)))
        qh_ref[h] = jnp.where(lane < HEAD_DIM, q_h.astype(F32), cols).astype(BF16)
    k_row = lax.broadcasted_iota(jnp.int32, (LANES - HEAD_DIM, blk), 0)
    ones_rows = jnp.where(k_row < 3, 1.0, 0.0).astype(BF16)

    def attend(slot, j, mask=None, first=False):
        pages = range(ch)
        heads = range(n_heads) if j is not None else ()
        start = pl.multiple_of(j * blk, blk) if j is not None else None
        zd = [jnp.dot(q_bd, kbuf[slot, i].reshape(width, page).astype(BF16), preferred_element_type=F32) + bias_d
              for i in pages]
        zp = [jnp.dot(qh_ref[h],
                      jnp.concatenate([kt_ref[0, h, :, pl.ds(start, blk)].astype(BF16), ones_rows], axis=0),
                      preferred_element_type=F32) for h in heads]
        spd = [_softplus(z) for z in zd]
        spp = [_softplus(z) for z in zp]
        lbd = [z - sp for z, sp in zip(zd, spd)]
        lbp = [z - sp for z, sp in zip(zp, spp)]
        rd = _suffix_sums(spd, negm_d, True)
        rp = _suffix_sums(spp if mask is None else [jnp.where(mask, sp, 0.0) for sp in spp], negm_p, two_pass)
        carry = dcarry_ref[...]
        ad = []
        for i in pages:
            ad.append(_stick_weights(lbd[i], rd[i], carry, None, page))
            carry = carry + rd[i][:, page:]
        dcarry_ref[...] = carry
        ap = []
        for h in heads:
            carry = None if first else carry_ref[h]
            ap.append(_stick_weights(lbp[h], rp[h], carry, mask, blk))
            carry_ref[h] = rp[h][:, blk:] if first else carry + rp[h][:, blk:]
        acc = dacc_ref[...]
        for i in pages:
            vt = vbuf[slot, i].reshape(width, page).astype(BF16)
            acc = acc + lax.dot_general(ad[i], vt, _NT, preferred_element_type=F32)
        dacc_ref[...] = acc
        for h in heads:
            vt = vt_ref[0, h, :, pl.ds(start, blk)].astype(BF16)
            pv = lax.dot_general(ap[h], vt, _NT, preferred_element_type=F32)
            acc_ref[h] = pv if first else acc_ref[h] + pv

    dcarry_ref[...], dacc_ref[...] = new_tokens()
    attend(fetch_chunk(0), qi, diag_mask, True)

    def step(c, _):
        slot = fetch_chunk(c)

        @pl.when(c <= qi)
        def _():
            attend(slot, qi - c)

        @pl.when(c > qi)
        def _():
            attend(slot, None)

        return 0

    lax.fori_loop(1, nc, step, 0)
    os_ref[0] = jnp.where(head_diag, dacc_ref[...], 0.0).reshape(n_heads, n_tok, width).sum(axis=0)
    o_ref[0] = jnp.concatenate([acc_ref[h] for h in range(n_heads)], axis=1).astype(o_ref.dtype)


def _attention(page_table, bias, q, kt4, vt4, q_s, k_s, v_s, kc, vc, blk, ch, ring):
    n, t, a = q.shape
    n_heads = kt4.shape[1]
    n_d, n_tok, _ = q_s.shape
    nq = t // blk
    n_pages = page_table.shape[1]
    assert LANES == 2 * HEAD_DIM, "two heads per lane group; the spare half of a head's group carries its bias"
    assert n_d == n * nq, "one decode sequence per prompt query block"
    assert nq <= n_pages // ch, "every older key block needs a decode chunk to follow"
    page_shape = kc.shape[1:]
    tok = pl.BlockSpec((1, n_tok, a), lambda b, i, pt: (b * nq + i, 0, 0))
    qblk = pl.BlockSpec((1, blk, a), lambda b, i, pt: (b, _query_block_order(i, nq), 0))
    return pl.pallas_call(
        functools.partial(_attn_kernel, blk=blk, n_pages=n_pages, ch=ch, two_pass=False),
        grid_spec=pltpu.PrefetchScalarGridSpec(
            num_scalar_prefetch=1,
            grid=(n, nq),
            in_specs=[pl.BlockSpec(memory_space=pltpu.SMEM), qblk,
                      pl.BlockSpec((1,) + kt4.shape[1:], lambda b, i, pt: (b, 0, 0, 0)),
                      pl.BlockSpec((1,) + vt4.shape[1:], lambda b, i, pt: (b, 0, 0, 0)),
                      tok, tok, tok,
                      pl.BlockSpec(memory_space=pl.ANY), pl.BlockSpec(memory_space=pl.ANY)],
            out_specs=[qblk, tok],
            scratch_shapes=[pltpu.VMEM((n_heads, blk, LANES), BF16),
                            pltpu.VMEM((n_heads, blk, LANES), F32),
                            pltpu.VMEM((n_heads, blk, HEAD_DIM), F32),
                            pltpu.VMEM((n_heads * n_tok, LANES), F32),
                            pltpu.VMEM((n_heads * n_tok, a), F32),
                            pltpu.VMEM((ring, ch) + page_shape, F32),
                            pltpu.VMEM((ring, ch) + page_shape, F32),
                            pltpu.SemaphoreType.DMA((2, ring))]),
        out_shape=[jax.ShapeDtypeStruct((n, t, a), BF16), jax.ShapeDtypeStruct((n_d, n_tok, a), F32)],
        compiler_params=pltpu.CompilerParams(dimension_semantics=("arbitrary", "arbitrary"),
                                             vmem_limit_bytes=VMEM_LIMIT),
        name="attention",
    )(page_table.reshape(-1), bias, q, kt4, vt4, q_s, k_s, v_s, kc, vc)


def _post_kernel(x_ref, attn_ref, u_ref, prev_ref, g1_ref, sh2_ref, sc2_ref, g2_ref, n2g_ref, fg_ref,
                 wpool_ref, pscale_ref, wo_ref, w1_ref, w2_ref, y_ref, ext_ref, *, pos0, zero_first_prev, ff_chunk,
                 final_norm):
    nb, tt, d = x_ref.shape
    m = nb * tt
    t_idx = pl.program_id(1)
    gw = wpool_ref.shape[1]

    u = u_ref[...]
    prev = prev_ref[...]
    if zero_first_prev:
        prev = jnp.where(t_idx > 0, prev, 0.0)
    ext_ref[:, 0:POOL_HALO, :] = prev
    ext_ref[:, POOL_HALO:POOL_HALO + tt, :] = u
    pos = pos0 + t_idx * tt + lax.broadcasted_iota(jnp.int32, (1, tt, gw), 1)
    parts = [attn_ref[...].reshape(m, attn_ref.shape[2]).astype(BF16)]
    for g, w in enumerate(POOL_WINDOWS):
        lo = g * gw
        ssum = ext_ref[:, POOL_HALO:POOL_HALO + tt, lo:lo + gw]
        for back in range(1, w):
            ssum = ssum + ext_ref[:, POOL_HALO - back:POOL_HALO - back + tt, lo:lo + gw]
        cnt = jnp.minimum(pos + 1, w).astype(F32)
        delta = (ssum / cnt - u[:, :, lo:lo + gw]).reshape(m, gw).astype(BF16)
        pooled = jnp.dot(delta, wpool_ref[g], preferred_element_type=F32) * pscale_ref[:, lo:lo + gw]
        parts.append(pooled.astype(BF16))
    mix = jnp.dot(jnp.concatenate(parts, axis=-1), wo_ref[...], preferred_element_type=F32)
    x1 = x_ref[...] + g1_ref[...] * mix.reshape(nb, tt, d)

    h2 = _norm_mod(x1, n2g_ref[...].reshape(1, 1, d), sc2_ref[...], sh2_ref[...]).reshape(m, d).astype(BF16)
    acc = jnp.zeros((m, d), F32)
    for c in range(w1_ref.shape[1] // ff_chunk):
        f = jnp.dot(h2, w1_ref[:, c * ff_chunk:(c + 1) * ff_chunk], preferred_element_type=F32)
        f = jnp.square(jnp.maximum(f, 0.0)).astype(BF16)
        acc = acc + jnp.dot(f, w2_ref[c * ff_chunk:(c + 1) * ff_chunk, :], preferred_element_type=F32)
    x2 = x1 + g2_ref[...] * acc.reshape(nb, tt, d)
    y_ref[...] = _rms(x2) * fg_ref[...].reshape(1, 1, d) if final_norm else x2


def _post(x, attn, u, prev, prev_map, mod, n2g, fg, wpool, pscale, wo, w1, w2, nb, tt, pos0, zero_first_prev,
          final_norm):
    n, t, d = x.shape
    a = attn.shape[2]
    pw = u.shape[2]
    tile = lambda width: pl.BlockSpec((nb, tt, width), lambda b, i: (b, i, 0))
    modspec = lambda k: pl.BlockSpec((nb, 1, d), lambda b, i: (b, 0, k))
    const2 = lambda b, i: (0, 0)
    resident = lambda arr: pl.BlockSpec(arr.shape, (lambda b, i: (0,) * arr.ndim), pipeline_mode=pl.Buffered(1))
    return pl.pallas_call(
        functools.partial(_post_kernel, pos0=pos0, zero_first_prev=zero_first_prev, ff_chunk=1024,
                          final_norm=final_norm),
        grid=(n // nb, t // tt),
        in_specs=[tile(d), tile(a), tile(pw),
                  pl.BlockSpec((nb, POOL_HALO, pw), prev_map),
                  modspec(2), modspec(3), modspec(4), modspec(5),
                  pl.BlockSpec((1, d), const2), pl.BlockSpec((1, d), const2),
                  resident(wpool), pl.BlockSpec((1, pw), const2),
                  resident(wo), resident(w1), resident(w2)],
        out_specs=tile(d),
        out_shape=jax.ShapeDtypeStruct((n, t, d), F32),
        scratch_shapes=[pltpu.VMEM((nb, POOL_HALO + tt, pw), F32)],
        compiler_params=pltpu.CompilerParams(dimension_semantics=("arbitrary", "arbitrary"),
                                             vmem_limit_bytes=VMEM_LIMIT),
        name="post",
    )(x, attn, u, prev, mod, mod, mod, mod, n2g, fg, wpool, pscale, wo, w1, w2)


def kernel(x_prompt, x_sample, c_prompt, c_sample, cache_k, cache_v, state_pool, page_table, norm1_g, norm2_g,
           w_ada, b_ada, w_in, sb_bias, w_pool, pool_scale, w_o, w_ff1, w_ff2, final_g):
    n_b, seq, d = x_prompt.shape
    n_d, dec_seq, _ = x_sample.shape
    depth = w_in.shape[0]
    n_heads = sb_bias.shape[1]
    a = n_heads * HEAD_DIM
    past_len = page_table.shape[1] * cache_k.shape[2]
    scale = HEAD_DIM ** -0.5
    fg = final_g.reshape(1, d)

    xp, xs = x_prompt, x_sample
    outs = [[] for _ in range(6)]
    for l in range(depth):
        mod = _modulation(jnp.concatenate([c_prompt, c_sample], axis=0), w_ada[l], b_ada[l].reshape(1, -1))
        mod_p = mod[:n_b].reshape(n_b, 1, -1)
        mod_s = mod[n_b:].reshape(n_d, 1, -1)
        g1 = norm1_g[l].reshape(1, d)
        g2 = norm2_g[l].reshape(1, d)
        w_in_b = w_in[l].astype(BF16)
        wq, wu = w_in_b[:, :a], w_in_b[:, 3 * a:]
        wkvt = w_in_b[:, a:3 * a].T
        wpool = w_pool[l].astype(BF16)
        pscale = pool_scale[l].reshape(1, -1)
        wo, w1, w2 = w_o[l].astype(BF16), w_ff1[l].astype(BF16), w_ff2[l].astype(BF16)
        bias = sb_bias[l]

        tt = 512
        q_p, kt_p, vt_p, u_p = _proj_prompt(xp, mod_p, g1, wq, wkvt, wu, tt, scale)
        kt4 = kt_p.reshape(n_b, n_heads, HEAD_DIM, seq)
        vt4 = vt_p.reshape(n_b, n_heads, HEAD_DIM, seq)
        q_s, k_s, v_s, u_s = _proj_sample(xs, mod_s, g1, w_in_b, 64, scale, a)
        kc = jnp.transpose(cache_k[l], (0, 2, 3, 1))
        vc = jnp.transpose(cache_v[l], (0, 2, 3, 1))
        attn_p, attn_s = _attention(page_table, bias, q_p, kt4, vt4, q_s, k_s, v_s, kc, vc, 256, 8, 6)

        halo_per_tile = tt // POOL_HALO
        xp = _post(xp, attn_p, u_p, u_p, lambda b, i: (b, jnp.maximum(i * halo_per_tile - 1, 0), 0),
                   mod_p, g2, fg, wpool, pscale, wo, w1, w2, 1, tt, 0, True, l == depth - 1)
        outs[0].append(jnp.transpose(kt4, (0, 3, 1, 2)))
        outs[1].append(jnp.transpose(vt4, (0, 3, 1, 2)))
        outs[2].append(u_p[:, seq - (POOL_HALO - 1):])

        prev_s = jnp.concatenate([jnp.zeros((n_d, 1, u_s.shape[2]), F32), state_pool[l]], axis=1)
        xs = _post(xs, attn_s, u_s, prev_s, lambda b, i: (b, 0, 0),
                   mod_s, g2, fg, wpool, pscale, wo, w1, w2, 64, dec_seq, past_len, False, l == depth - 1)
        outs[3].append(k_s.reshape(n_d, dec_seq, n_heads, HEAD_DIM))
        outs[4].append(v_s.reshape(n_d, dec_seq, n_heads, HEAD_DIM))
        outs[5].append(jnp.concatenate([state_pool[l], u_s], axis=1)[:, -(POOL_HALO - 1):])

    stk = [jnp.stack(o) for o in outs]
    return (xp, xs, stk[0], stk[1], stk[2], stk[3], stk[4], stk[5])
```
